```python
import jax, jax.numpy as jnp
from jax import lax
import numpy as np

D_MODEL = 2048
BATCH = 16
SEQ = 256
DEPTH = 2
DEC_BATCH = 8
DEC_SEQ = 1024
PAST_LEN = 256

GRID_W = 64
HEAD_DIM = 64
MIX_WIDTH = D_MODEL
BRANCH_W = MIX_WIDTH // 4
FOURIER_GROUPS = 8
FOURIER_GROUP_W = BRANCH_W // FOURIER_GROUPS
HYENA_W = BRANCH_W
SHORT_CONV = 3
FILTER_FREQS = 16
FILTER_FEATS = 1 + 2 * FILTER_FREQS
FILTER_HIDDEN = 64
DECAY_SHIFT = 0.05
N_HEADS = BRANCH_W // HEAD_DIM
N_KV_HEADS = 2
KV_W = N_KV_HEADS * HEAD_DIM
WINDOW = 128
BLOCK = 128
ROPE_THETA = 10000.0
ALPHA = (2 * DEPTH) ** 0.25
BETA = (8 * DEPTH) ** -0.25
LN_EPS = 1e-5
RMS_EPS = 1e-6
NEG_INF = -1e30
IN_SIZES = (BRANCH_W, BRANCH_W, 3 * HYENA_W, HYENA_W,
            BRANCH_W, KV_W, KV_W, BRANCH_W,
            BRANCH_W, KV_W, KV_W, BRANCH_W)
IN_WIDTH = sum(IN_SIZES)

kernel_name = 'hybrid_fourier_hyena_swa_gqa_prefix_dit'

F32 = jnp.float32


def split_points(sizes):
    pts, acc = [], 0
    for s in sizes[:-1]:
        acc += s
        pts.append(acc)
    return pts


def layer_norm(x, g=None, b=None):
    xf = x.astype(F32)
    mu = jnp.mean(xf, -1, keepdims=True)
    var = jnp.mean(jnp.square(xf - mu), -1, keepdims=True)
    y = (xf - mu) * lax.rsqrt(var + LN_EPS)
    if g is not None:
        y = y * g.astype(F32) + b.astype(F32)
    return y.astype(x.dtype)


def rms_norm(x, g):
    xf = x.astype(F32)
    y = xf * lax.rsqrt(jnp.mean(xf * xf, -1, keepdims=True) + RMS_EPS) * g.astype(F32)
    return y.astype(x.dtype)


def axial_rope_angles(seq_len):
    rows = seq_len // GRID_W
    row = jnp.repeat(jnp.arange(rows, dtype=F32), GRID_W)
    col = jnp.tile(jnp.arange(GRID_W, dtype=F32), rows)
    half = HEAD_DIM // 2
    inv = ROPE_THETA ** (-jnp.arange(0, half, 2, dtype=F32) / half)
    return row[:, None] * inv, col[:, None] * inv


def rotate(x, ang):
    m = ang.shape[-1]
    cos = jnp.cos(ang)[None, :, None, :].astype(x.dtype)
    sin = jnp.sin(ang)[None, :, None, :].astype(x.dtype)
    x1, x2 = x[..., :m], x[..., m:]
    return jnp.concatenate([x1 * cos - x2 * sin, x2 * cos + x1 * sin], -1)


def axial_rope(x, ang_row, ang_col):
    half = HEAD_DIM // 2
    return jnp.concatenate([rotate(x[..., :half], ang_row), rotate(x[..., half:], ang_col)], -1)


def modulation(cond, w_ada, b_ada):
    m = jax.nn.silu(cond) @ w_ada + b_ada
    shift, scale, gate = jnp.split(m, 3, axis=-1)
    return shift[:, None, :], scale[:, None, :], gate[:, None, :]


def fourier_mix(a, w_f):
    b, l, _ = a.shape
    a4 = a.reshape(b, l, FOURIER_GROUPS, FOURIER_GROUP_W).astype(F32)
    f = jnp.fft.fft2(a4, axes=(1, 3), norm='ortho').real
    return f.reshape(b, l, BRANCH_W).astype(a.dtype) @ w_f


def short_conv3(u, w, b):
    up = jnp.pad(u, ((0, 0), (1, 1), (0, 0)))
    return up[:, :-2] * w[0] + up[:, 1:-1] * w[1] + up[:, 2:] * w[2] + b


def hyena_filters(seq_len, w1, b1, w2, b2, w3, log_decay):
    t = jnp.arange(seq_len, dtype=F32)
    tn = t / seq_len
    freqs = jnp.arange(1, FILTER_FREQS + 1, dtype=F32)
    ang = 2.0 * jnp.pi * tn[:, None] * freqs[None, :]
    feats = jnp.concatenate([tn[:, None], jnp.cos(ang), jnp.sin(ang)], -1)
    h = jnp.sin(feats @ w1.astype(F32) + b1.astype(F32))
    h = jnp.sin(h @ w2.astype(F32) + b2.astype(F32))
    h = h @ w3.astype(F32)
    dist = jnp.abs(t - seq_len // 2) / (seq_len / 2)
    window = jnp.exp(-dist[:, None] * jnp.exp(log_decay.astype(F32))[None, :]) + DECAY_SHIFT
    h = h * window
    return h * lax.rsqrt(jnp.sum(h * h, 0, keepdims=True) + RMS_EPS)


def long_conv(u, h, skip):
    l = u.shape[1]
    n = 2 * l
    c0 = l // 2
    uf = jnp.fft.rfft(u.astype(F32), n=n, axis=1)
    hf = jnp.fft.rfft(h, n=n, axis=0)
    z = jnp.fft.irfft(uf * hf[None], n=n, axis=1)[:, c0:c0 + l]
    return (z + u.astype(F32) * skip.astype(F32)).astype(u.dtype)


def hyena_mix(u, conv_w, conv_b, fw1, fb1, fw2, fb2, fw3, log_decay, skip):
    u = short_conv3(u, conv_w, conv_b)
    v, x1, x2 = jnp.split(u, 3, axis=-1)
    filt = hyena_filters(u.shape[1], fw1, fb1, fw2, fb2, fw3, log_decay)
    z = x1 * long_conv(v, filt[:, :HYENA_W], skip[:HYENA_W])
    return x2 * long_conv(z, filt[:, HYENA_W:], skip[HYENA_W:])


def to_heads(x, n):
    return x.reshape(x.shape[0], x.shape[1], n, HEAD_DIM)


def dense_block_attention(q, k, v, sink):
    b, lq, h, d = q.shape
    kvh = k.shape[2]
    g = h // kvh
    nb = lq // BLOCK
    lk = k.shape[1]
    scale = d ** -0.5
    qb = q.reshape(b, nb, BLOCK, kvh, g, d).swapaxes(0, 1)

    def one_block(qblk):
        s = jnp.einsum('bqkgd,bskd->bkgqs', qblk, k, preferred_element_type=F32) * scale
        if sink is not None:
            sk = jnp.broadcast_to(sink.astype(F32).reshape(kvh, g)[None, :, :, None, None],
                                  s.shape[:-1] + (1,))
            s = jnp.concatenate([s, sk], -1)
        w = jax.nn.softmax(s, axis=-1)[..., :lk]
        return jnp.einsum('bkgqs,bskd->bqkgd', w.astype(v.dtype), v)

    o = lax.map(one_block, qb)
    return o.swapaxes(0, 1).reshape(b, lq, h * d)


def windowed_attention(q, k, v, k_ctx, v_ctx, sink):
    b, l, h, d = q.shape
    kvh = k.shape[2]
    g = h // kvh
    nb = l // BLOCK
    span = BLOCK + 2 * WINDOW
    lc = k_ctx.shape[1]
    scale = d ** -0.5
    pad = ((0, 0), (WINDOW, WINDOW), (0, 0), (0, 0))
    idx = jnp.arange(nb)[:, None] * BLOCK + jnp.arange(span)[None, :]
    kb = jnp.pad(k, pad)[:, idx]
    vb = jnp.pad(v, pad)[:, idx]
    qb = q.reshape(b, nb, BLOCK, kvh, g, d)
    s_loc = jnp.einsum('bnqkgd,bnskd->bnkgqs', qb, kb, preferred_element_type=F32) * scale
    qpos = jnp.arange(nb)[:, None, None] * BLOCK + jnp.arange(BLOCK)[None, :, None]
    kpos = jnp.arange(nb)[:, None, None] * BLOCK + jnp.arange(span)[None, None, :] - WINDOW
    valid = (jnp.abs(kpos - qpos) <= WINDOW) & (kpos >= 0) & (kpos < l)
    s_loc = jnp.where(valid[None, :, None, None], s_loc, NEG_INF)
    s_ctx = jnp.einsum('bnqkgd,bckd->bnkgqc', qb, k_ctx.astype(qb.dtype), preferred_element_type=F32) * scale
    s_sink = jnp.broadcast_to(sink.astype(F32).reshape(kvh, g)[None, None, :, :, None, None],
                              s_loc.shape[:-1] + (1,))
    w = jax.nn.softmax(jnp.concatenate([s_loc, s_ctx, s_sink], -1), axis=-1).astype(v.dtype)
    o = (jnp.einsum('bnkgqs,bnskd->bnqkgd', w[..., :span], vb)
         + jnp.einsum('bnkgqc,bckd->bnqkgd', w[..., span:span + lc], v_ctx.astype(v.dtype)))
    return o.reshape(b, l, h * d)


def trunk_layer(x, mod, p, ctx_kv=None, rope=None):
    shift, scale, gate = mod
    hmod = layer_norm(x) * (1 + scale) + shift
    parts = jnp.split(hmod @ p['w_in'], split_points(IN_SIZES), axis=-1)
    a_in, a_g, b_in, b_g, cq, ck, cv, c_g, dq, dk, dv, d_g = parts
    y_a = fourier_mix(a_in, p['w_fourier']) * jax.nn.silu(a_g)
    y_b = hyena_mix(b_in, p['conv_w'], p['conv_b'], p['filt_w1'], p['filt_b1'], p['filt_w2'],
                    p['filt_b2'], p['filt_w3'], p['filt_log_decay'], p['hyena_skip']) * jax.nn.silu(b_g)
    cq, ck, cv = to_heads(cq, N_HEADS), to_heads(ck, N_KV_HEADS), to_heads(cv, N_KV_HEADS)
    dq = rms_norm(to_heads(dq, N_HEADS), p['q_norm'])
    dk = rms_norm(to_heads(dk, N_KV_HEADS), p['k_norm'])
    dv = to_heads(dv, N_KV_HEADS)
    if ctx_kv is None:
        o_c = dense_block_attention(cq, ck, cv, p['sink'])
        o_d = dense_block_attention(dq, dk, dv, None)
        new_kv = (ck, cv, dk, dv)
    else:
        ang_r, ang_c = rope
        kc_ctx, vc_ctx, kd_ctx, vd_ctx = ctx_kv
        o_c = windowed_attention(axial_rope(cq, ang_r, ang_c), axial_rope(ck, ang_r, ang_c), cv,
                                 kc_ctx, vc_ctx, p['sink'])
        k_all = jnp.concatenate([axial_rope(dk, ang_r, ang_c), kd_ctx.astype(dk.dtype)], axis=1)
        v_all = jnp.concatenate([dv, vd_ctx.astype(dv.dtype)], axis=1)
        o_d = dense_block_attention(axial_rope(dq, ang_r, ang_c), k_all, v_all, None)
        new_kv = None
    y = jnp.concatenate([y_a, y_b, o_c * jax.nn.silu(c_g), o_d * jax.nn.silu(d_g)], -1) @ p['w_out']
    x = layer_norm(ALPHA * x + gate * y, p['ln_g'], p['ln_b'])
    return x, new_kv


def setup_inputs(seed: int = 0) -> dict:
    key = jax.random.key(seed)
    ks = jax.random.split(key, 28)

    def nrm(k, shape, s):
        return jax.random.normal(k, shape, F32) * s

    cache_shape = (DEC_BATCH, DEPTH, PAST_LEN, N_KV_HEADS, HEAD_DIM)
    base_decay = jnp.log(jnp.linspace(3.0, 15.0, 2 * HYENA_W, dtype=F32))
    return {
        'x_prompt': nrm(ks[0], (BATCH, SEQ, D_MODEL), 1.0),
        'x_sample': nrm(ks[1], (DEC_BATCH, DEC_SEQ, D_MODEL), 1.0),
        'cache_attn_c_k': nrm(ks[2], cache_shape, 1.0),
        'cache_attn_c_v': nrm(ks[3], cache_shape, 1.0),
        'cache_attn_d_k': nrm(ks[4], cache_shape, 1.0),
        'cache_attn_d_v': nrm(ks[5], cache_shape, 1.0),
        'c': nrm(ks[6], (DEC_BATCH, D_MODEL), 1.0),
        'c_ctx': nrm(ks[7], (D_MODEL,), 1.0),
        'w_ada': nrm(ks[8], (DEPTH, D_MODEL, 3 * D_MODEL), 0.5 * D_MODEL ** -0.5),
        'b_ada': nrm(ks[9], (DEPTH, 3 * D_MODEL), 0.01),
        'w_in': nrm(ks[10], (DEPTH, D_MODEL, IN_WIDTH), D_MODEL ** -0.5),
        'w_fourier': nrm(ks[11], (DEPTH, BRANCH_W, BRANCH_W), BRANCH_W ** -0.5),
        'conv_w': nrm(ks[12], (DEPTH, SHORT_CONV, 3 * HYENA_W), SHORT_CONV ** -0.5),
        'conv_b': nrm(ks[13], (DEPTH, 3 * HYENA_W), 0.01),
        'filt_w1': nrm(ks[14], (DEPTH, FILTER_FEATS, FILTER_HIDDEN), 1.0),
        'filt_b1': nrm(ks[15], (DEPTH, FILTER_HIDDEN), 0.5),
        'filt_w2': nrm(ks[16], (DEPTH, FILTER_HIDDEN, FILTER_HIDDEN), FILTER_HIDDEN ** -0.5),
        'filt_b2': nrm(ks[17], (DEPTH, FILTER_HIDDEN), 0.5),
        'filt_w3': nrm(ks[18], (DEPTH, FILTER_HIDDEN, 2 * HYENA_W), FILTER_HIDDEN ** -0.5),
        'filt_log_decay': base_decay[None, :] + nrm(ks[19], (DEPTH, 2 * HYENA_W), 0.01),
        'hyena_skip': nrm(ks[20], (DEPTH, 2 * HYENA_W), 0.5),
        'sink_logit': nrm(ks[21], (DEPTH, N_HEADS), 0.5),
        'q_norm': 1.0 + nrm(ks[22], (DEPTH, HEAD_DIM), 0.01),
        'k_norm': 1.0 + nrm(ks[23], (DEPTH, HEAD_DIM), 0.01),
        'w_out': nrm(ks[24], (DEPTH, MIX_WIDTH, D_MODEL), BETA * MIX_WIDTH ** -0.5),
        'ln_g': 1.0 + nrm(ks[25], (DEPTH, D_MODEL), 0.01),
        'ln_b': nrm(ks[26], (DEPTH, D_MODEL), 0.01),
    }


def reference(x_prompt, x_sample, cache_attn_c_k, cache_attn_c_v, cache_attn_d_k, cache_attn_d_v,
              c, c_ctx, w_ada, b_ada, w_in, w_fourier, conv_w, conv_b, filt_w1, filt_b1, filt_w2,
              filt_b2, filt_w3, filt_log_decay, hyena_skip, sink_logit, q_norm, k_norm, w_out,
              ln_g, ln_b):
    rope = axial_rope_angles(x_sample.shape[1])
    y_p, y_s = x_prompt, x_sample
    kc_list, vc_list, kd_list, vd_list = [], [], [], []
    for l in range(DEPTH):
        p = {
            'w_in': w_in[l], 'w_fourier': w_fourier[l], 'conv_w': conv_w[l], 'conv_b': conv_b[l],
            'filt_w1': filt_w1[l], 'filt_b1': filt_b1[l], 'filt_w2': filt_w2[l],
            'filt_b2': filt_b2[l], 'filt_w3': filt_w3[l], 'filt_log_decay': filt_log_decay[l],
            'hyena_skip': hyena_skip[l], 'sink': sink_logit[l], 'q_norm': q_norm[l],
            'k_norm': k_norm[l], 'w_out': w_out[l], 'ln_g': ln_g[l], 'ln_b': ln_b[l],
        }
        mod_ctx = modulation(c_ctx[None, :], w_ada[l], b_ada[l])
        mod_lat = modulation(c, w_ada[l], b_ada[l])
        y_p, kv = trunk_layer(y_p, mod_ctx, p)
        kc_list.append(kv[0])
        vc_list.append(kv[1])
        kd_list.append(kv[2])
        vd_list.append(kv[3])
        ctx_kv = (cache_attn_c_k[:, l], cache_attn_c_v[:, l], cache_attn_d_k[:, l], cache_attn_d_v[:, l])
        y_s, _ = trunk_layer(y_s, mod_lat, p, ctx_kv=ctx_kv, rope=rope)
    new_c_k = jnp.stack(kc_list, axis=1)
    new_c_v = jnp.stack(vc_list, axis=1)
    new_d_k = jnp.stack(kd_list, axis=1)
    new_d_v = jnp.stack(vd_list, axis=1)
    return (y_p, y_s, new_c_k, new_c_v, new_d_k, new_d_v)
```

```python
import functools

import numpy as np
import jax
import jax.numpy as jnp
from jax import lax
from jax.experimental import pallas as pl
from jax.experimental.pallas import tpu as pltpu

D_MODEL = 2048
DEPTH = 2
GRID_W = 64
HEAD_DIM = 64
BRANCH_W = 512
FOURIER_GROUPS = 8
FOURIER_GROUP_W = BRANCH_W // FOURIER_GROUPS
FILTER_FREQS = 16
FILTER_FEATS = 1 + 2 * FILTER_FREQS
FILTER_HIDDEN = 64
DECAY_SHIFT = 0.05
N_HEADS = 8
N_KV_HEADS = 2
GROUP = N_HEADS // N_KV_HEADS
KV_W = N_KV_HEADS * HEAD_DIM
WINDOW = 128
ROPE_THETA = 10000.0
ALPHA = (2 * DEPTH) ** 0.25
LN_EPS = 1e-5
RMS_EPS = 1e-6
NEG_INF = -1e30
IN_WIDTH = 5632
SCALE = HEAD_DIM ** -0.5

F32 = jnp.float32
BF16 = jnp.bfloat16

VMEM_LIMIT_BYTES = 56 * 1024 * 1024
LANES = 128

COL_A_IN, COL_A_G, COL_B_V, COL_B_X1, COL_B_X2, COL_B_G = 0, 1, 2, 3, 4, 5
COL_CQ, COL_CG, COL_DQ, COL_DG, COL_KV = 6, 7, 8, 9, 10
MOD_ROWS = 16


def _cparams(*sem):
    return pltpu.CompilerParams(dimension_semantics=sem, vmem_limit_bytes=VMEM_LIMIT_BYTES)


def _silu(x):
    return x * jax.nn.sigmoid(x)


def _dot(a, b):
    return jnp.dot(a, b, preferred_element_type=F32)


def _dot_exact(a, b):
    return jnp.dot(a, b, preferred_element_type=F32, precision=lax.Precision.HIGHEST)


@functools.lru_cache(maxsize=None)
def _fourier_pos_table(l):
    k = np.arange(l, dtype=np.int64)
    ang = 2.0 * np.pi * ((k[:, None] * k[None, :]) % l) / l
    return (np.concatenate([np.cos(ang), -np.sin(ang)], axis=1) / np.sqrt(l)).astype(np.float32)


@functools.lru_cache(maxsize=None)
def _fourier_group_tables():
    w = FOURIER_GROUP_W
    k = np.arange(w, dtype=np.int64)
    ang = 2.0 * np.pi * ((k[:, None] * k[None, :]) % w) / w
    eye = np.eye(FOURIER_GROUPS)
    c = np.kron(eye, np.cos(ang)) / np.sqrt(w)
    s = np.kron(eye, np.sin(ang)) / np.sqrt(w)
    return np.stack([c, s]).astype(np.float32)


@functools.lru_cache(maxsize=None)
def _conv_dft_tables(l):
    n = 2 * l
    k = np.arange(l, dtype=np.int64)
    s = np.arange(l, dtype=np.int64)
    ang = 2.0 * np.pi * ((k[:, None] * s[None, :]) % n) / n
    fr = np.cos(ang)
    fi = -np.sin(ang)
    fi[0, :] = np.where(s % 2 == 0, 1.0, -1.0)
    fwd = np.concatenate([fr, fi], axis=0)
    t = np.arange(l, dtype=np.int64) + l // 2
    ang_i = 2.0 * np.pi * ((t[:, None] * k[None, :]) % n) / n
    wk = np.where(k == 0, 1.0, 2.0) / n
    gr = np.cos(ang_i) * wk[None, :]
    gi = -np.sin(ang_i) * wk[None, :]
    gi[:, 0] = np.where(t % 2 == 0, 1.0, -1.0) / n
    inv = np.concatenate([gr, gi], axis=1)
    return fwd.astype(np.float32), inv.astype(np.float32)


@functools.lru_cache(maxsize=None)
def _filter_feats(l):
    t = np.arange(l, dtype=np.float64) / l
    f = np.arange(1, FILTER_FREQS + 1, dtype=np.float64)
    ang = 2.0 * np.pi * t[:, None] * f[None, :]
    feats = np.concatenate([t[:, None], np.cos(ang), np.sin(ang)], axis=1)
    out = np.zeros((l, FILTER_HIDDEN), np.float64)
    out[:, :FILTER_FEATS] = feats
    return out.astype(np.float32)


@functools.lru_cache(maxsize=None)
def _rope_tables(l):
    t = np.arange(l)
    row = (t // GRID_W).astype(np.float64)
    col = (t % GRID_W).astype(np.float64)
    half = HEAD_DIM // 2
    inv = ROPE_THETA ** (-np.arange(0, half, 2, dtype=np.float64) / half)
    ar = row[:, None] * inv
    ac = col[:, None] * inv
    cos_h = np.concatenate([np.cos(ar), np.cos(ar), np.cos(ac), np.cos(ac)], axis=1)
    zero = np.zeros_like(ar)
    sin_up = np.concatenate([-np.sin(ar), zero, -np.sin(ac), zero], axis=1)
    sin_dn = np.concatenate([zero, np.sin(ar), zero, np.sin(ac)], axis=1)
    tabs = [np.tile(x, (1, LANES // HEAD_DIM)) for x in (cos_h, sin_up, sin_dn)]
    return np.stack(tabs).astype(np.float32)


def _mod_kernel(cond_ref, w_ref, b_ref, o_ref):
    s = _silu(cond_ref[...])
    o_ref[0] = _dot(s.astype(BF16), w_ref[0].astype(BF16)) + b_ref[0]


def _modulation(cond, w_ada, b_ada):
    tn = 512
    n = 3 * D_MODEL
    out = pl.pallas_call(
        _mod_kernel,
        grid=(DEPTH, n // tn),
        in_specs=[
            pl.BlockSpec((MOD_ROWS, D_MODEL), lambda l, j: (0, 0)),
            pl.BlockSpec((1, D_MODEL, tn), lambda l, j: (l, 0, j)),
            pl.BlockSpec((1, 1, tn), lambda l, j: (l, 0, j)),
        ],
        out_specs=pl.BlockSpec((1, MOD_ROWS, tn), lambda l, j: (l, 0, j)),
        out_shape=jax.ShapeDtypeStruct((DEPTH, MOD_ROWS, n), F32),
        compiler_params=_cparams("arbitrary", "arbitrary"),
        name="modulation",
    )(cond, w_ada, b_ada.reshape(DEPTH, 1, n))
    return out.reshape(DEPTH, MOD_ROWS, 3, D_MODEL)


def _layer_norm_rows(x):
    mu = jnp.mean(x, axis=-1, keepdims=True)
    xc = x - mu
    var = jnp.mean(xc * xc, axis=-1, keepdims=True)
    return xc * lax.rsqrt(var + LN_EPS)


def _inproj_kernel(x_ref, mod_ref, w_ref, o_ref, hm_ref):
    @pl.when(pl.program_id(2) == 0)
    def _():
        y = _layer_norm_rows(x_ref[0])
        hm = y * (1.0 + mod_ref[0, 1:2, :]) + mod_ref[0, 0:1, :]
        hm_ref[...] = hm.astype(BF16)

    o_ref[0] = _dot(hm_ref[...], w_ref[...])


def _in_projection(x, mod, mod_row, w_in_bf16):
    b, l, _ = x.shape
    tl = min(l, 1024)
    tn = 512
    return pl.pallas_call(
        _inproj_kernel,
        grid=(b, l // tl, IN_WIDTH // tn),
        in_specs=[
            pl.BlockSpec((1, tl, D_MODEL), lambda bi, i, j: (bi, i, 0)),
            pl.BlockSpec((1, 3, D_MODEL), lambda bi, i, j: (mod_row(bi), 0, 0)),
            pl.BlockSpec((D_MODEL, tn), lambda bi, i, j: (0, j)),
        ],
        out_specs=pl.BlockSpec((1, tl, tn), lambda bi, i, j: (bi, i, j)),
        out_shape=jax.ShapeDtypeStruct((b, l, IN_WIDTH), F32),
        scratch_shapes=[pltpu.VMEM((tl, D_MODEL), BF16)],
        compiler_params=_cparams("arbitrary", "arbitrary", "arbitrary"),
        name="in_projection",
    )(x, mod, w_in_bf16)


def _outproj_kernel(ya_ref, yb_ref, oc_ref, od_ref, x_ref, mod_ref, w_ref, g_ref, b_ref, o_ref):
    y = _dot(ya_ref[0], w_ref[0])
    y += _dot(yb_ref[0], w_ref[1])
    y += _dot(oc_ref[0], w_ref[2])
    y += _dot(od_ref[0], w_ref[3])
    z = ALPHA * x_ref[0] + mod_ref[0, 2:3, :] * y
    o_ref[0] = _layer_norm_rows(z) * g_ref[...] + b_ref[...]


def _out_projection(parts, x, mod, mod_row, w_out_bf16, ln_g, ln_b):
    b, l, _ = x.shape
    tl = min(l, 512)
    part_spec = pl.BlockSpec((1, tl, BRANCH_W), lambda bi, i: (bi, i, 0))
    row_spec = pl.BlockSpec((1, tl, D_MODEL), lambda bi, i: (bi, i, 0))
    vec_spec = pl.BlockSpec((1, D_MODEL), lambda bi, i: (0, 0))
    return pl.pallas_call(
        _outproj_kernel,
        grid=(b, l // tl),
        in_specs=[
            part_spec, part_spec, part_spec, part_spec, row_spec,
            pl.BlockSpec((1, 3, D_MODEL), lambda bi, i: (mod_row(bi), 0, 0)),
            pl.BlockSpec((4, BRANCH_W, D_MODEL), lambda bi, i: (0, 0, 0)),
            vec_spec, vec_spec,
        ],
        out_specs=row_spec,
        out_shape=jax.ShapeDtypeStruct((b, l, D_MODEL), F32),
        compiler_params=_cparams("arbitrary", "arbitrary"),
        name="out_projection",
    )(*parts, x, mod, w_out_bf16, ln_g.reshape(1, D_MODEL), ln_b.reshape(1, D_MODEL))


def _fourier_weight_kernel(t_ref, w_ref, o_ref):
    w = w_ref[0]
    o_ref[0, :, :BRANCH_W] = _dot_exact(t_ref[0], w).astype(BF16)
    o_ref[0, :, BRANCH_W:] = _dot_exact(t_ref[1], w).astype(BF16)


def _fourier_weights(w_fourier):
    return pl.pallas_call(
        _fourier_weight_kernel,
        grid=(DEPTH,),
        in_specs=[
            pl.BlockSpec((2, BRANCH_W, BRANCH_W), lambda l: (0, 0, 0)),
            pl.BlockSpec((1, BRANCH_W, BRANCH_W), lambda l: (l, 0, 0)),
        ],
        out_specs=pl.BlockSpec((1, BRANCH_W, 2 * BRANCH_W), lambda l: (l, 0, 0)),
        out_shape=jax.ShapeDtypeStruct((DEPTH, BRANCH_W, 2 * BRANCH_W), BF16),
        compiler_params=_cparams("arbitrary"),
        name="fourier_weights",
    )(jnp.asarray(_fourier_group_tables()), w_fourier)


def _fourier_kernel(a_ref, g_ref, w_ref, t_ref, o_ref):
    pq = _dot(a_ref[0].astype(BF16), w_ref[...])
    pq = jnp.concatenate([pq[:, :BRANCH_W], pq[:, BRANCH_W:]], axis=0).astype(BF16)
    y = _dot(t_ref[...], pq)
    o_ref[0] = (y * _silu(g_ref[0])).astype(BF16)


def _fourier_branch(h, w_cs, pos_table_bf16):
    b, l, _ = h.shape
    return pl.pallas_call(
        _fourier_kernel,
        grid=(b,),
        in_specs=[
            pl.BlockSpec((1, l, BRANCH_W), lambda bi: (bi, 0, COL_A_IN)),
            pl.BlockSpec((1, l, BRANCH_W), lambda bi: (bi, 0, COL_A_G)),
            pl.BlockSpec((BRANCH_W, 2 * BRANCH_W), lambda bi: (0, 0)),
            pl.BlockSpec((l, 2 * l), lambda bi: (0, 0)),
        ],
        out_specs=pl.BlockSpec((1, l, BRANCH_W), lambda bi: (bi, 0, 0)),
        out_shape=jax.ShapeDtypeStruct((b, l, BRANCH_W), BF16),
        compiler_params=_cparams("arbitrary"),
        name="fourier_branch",
    )(h, h, w_cs, pos_table_bf16)


def _hyena_filter_kernel(feat_ref, w1_ref, b1_ref, w2_ref, b2_ref, w3_ref, dec_ref, fwd_ref, o_ref):
    l = feat_ref.shape[0]
    h = jnp.sin(_dot_exact(feat_ref[...], w1_ref[0]) + b1_ref[0])
    h = jnp.sin(_dot_exact(h, w2_ref[0]) + b2_ref[0])
    h = _dot_exact(h, w3_ref[0])
    t = lax.broadcasted_iota(jnp.int32, h.shape, 0)
    dist = jnp.abs(t - l // 2).astype(F32) * (1.0 / (l / 2))
    window = jnp.exp(-dist * jnp.exp(dec_ref[0])) + DECAY_SHIFT
    h = h * window
    h = h * lax.rsqrt(jnp.sum(h * h, axis=0, keepdims=True) + RMS_EPS)
    spec = _dot_exact(fwd_ref[...], h)
    hr = spec[:l]
    hi = spec[l:]
    first = t == 0
    o_ref[0, 0] = hr
    o_ref[0, 1] = jnp.where(first, 0.0, hi)
    o_ref[0, 2] = jnp.where(first, hi, hr)


def _hyena_filters(l, fw1, fb1, fw2, fb2, fw3, log_decay, fwd_table):
    c = 2 * BRANCH_W
    tc = 256
    w1 = jnp.pad(fw1, ((0, 0), (0, FILTER_HIDDEN - FILTER_FEATS), (0, 0)))
    vec = lambda n: pl.BlockSpec((1, 1, n), lambda d, j: (d, 0, 0))
    mat = lambda m, n: pl.BlockSpec((1, m, n), lambda d, j: (d, 0, 0))
    return pl.pallas_call(
        _hyena_filter_kernel,
        grid=(DEPTH, c // tc),
        in_specs=[
            pl.BlockSpec((l, FILTER_HIDDEN), lambda d, j: (0, 0)),
            mat(FILTER_HIDDEN, FILTER_HIDDEN), vec(FILTER_HIDDEN),
            mat(FILTER_HIDDEN, FILTER_HIDDEN), vec(FILTER_HIDDEN),
            pl.BlockSpec((1, FILTER_HIDDEN, tc), lambda d, j: (d, 0, j)),
            pl.BlockSpec((1, 1, tc), lambda d, j: (d, 0, j)),
            pl.BlockSpec((2 * l, l), lambda d, j: (0, 0)),
        ],
        out_specs=pl.BlockSpec((1, 3, l, tc), lambda d, j: (d, 0, 0, j)),
        out_shape=jax.ShapeDtypeStruct((DEPTH, 3, l, c), F32),
        compiler_params=_cparams("arbitrary", "arbitrary"),
        name="hyena_filters",
    )(jnp.asarray(_filter_feats(l)), w1, fb1.reshape(DEPTH, 1, -1), fw2, fb2.reshape(DEPTH, 1, -1),
      fw3, log_decay.reshape(DEPTH, 1, c), fwd_table)


def _short_conv(u, w, b):
    l = u.shape[0]
    t = lax.broadcasted_iota(jnp.int32, u.shape, 0)
    prev = jnp.where(t == 0, 0.0, pltpu.roll(u, 1, axis=0))
    nxt = jnp.where(t == l - 1, 0.0, pltpu.roll(u, l - 1, axis=0))
    return prev * w[0:1] + u * w[1:2] + nxt * w[2:3] + b


def _long_conv(u, filt_ref, skip, fwd_ref, inv_ref):
    l = u.shape[0]
    spec = _dot(fwd_ref[...], u.astype(BF16))
    ur, ui = spec[:l], spec[l:]
    hr, hi, hn = filt_ref[0, 0], filt_ref[0, 1], filt_ref[0, 2]
    zr = ur * hr - ui * hi
    zi = ur * hi + ui * hn
    z = jnp.concatenate([zr, zi], axis=0).astype(BF16)
    return _dot(inv_ref[...], z) + u * skip


def _hyena_kernel(v_ref, x1_ref, x2_ref, g_ref, cwv_ref, cwx1_ref, cwx2_ref, cbv_ref, cbx1_ref,
                  cbx2_ref, f1_ref, f2_ref, s1_ref, s2_ref, fwd_ref, inv_ref, o_ref):
    v = _short_conv(v_ref[0], cwv_ref[...], cbv_ref[...])
    x1 = _short_conv(x1_ref[0], cwx1_ref[...], cbx1_ref[...])
    z = x1 * _long_conv(v, f1_ref, s1_ref[...], fwd_ref, inv_ref)
    x2 = _short_conv(x2_ref[0], cwx2_ref[...], cbx2_ref[...])
    y = x2 * _long_conv(z, f2_ref, s2_ref[...], fwd_ref, inv_ref)
    o_ref[0] = (y * _silu(g_ref[0])).astype(BF16)


def _hyena_branch(h, conv_w, conv_b, filt, layer, skip, fwd_bf16, inv_bf16):
    b, l, _ = h.shape
    tc = 256
    nc = BRANCH_W // tc
    per = BRANCH_W // tc
    col = lambda blk: pl.BlockSpec((1, l, tc), lambda j, bi: (bi, 0, blk * per + j))
    cw = lambda part: pl.BlockSpec((3, tc), lambda j, bi: (0, part * per + j))
    cb = lambda part: pl.BlockSpec((1, tc), lambda j, bi: (0, part * per + j))
    flt = lambda which: pl.BlockSpec((1, 3, l, tc), lambda j, bi: (layer, 0, 0, which * per + j))
    skp = lambda which: pl.BlockSpec((1, tc), lambda j, bi: (0, which * per + j))
    conv_b2 = conv_b.reshape(1, -1)
    skip2 = skip.reshape(1, -1)
    return pl.pallas_call(
        _hyena_kernel,
        grid=(nc, b),
        in_specs=[
            col(COL_B_V), col(COL_B_X1), col(COL_B_X2), col(COL_B_G),
            cw(0), cw(1), cw(2), cb(0), cb(1), cb(2),
            flt(0), flt(1), skp(0), skp(1),
            pl.BlockSpec((2 * l, l), lambda j, bi: (0, 0)),
            pl.BlockSpec((l, 2 * l), lambda j, bi: (0, 0)),
        ],
        out_specs=pl.BlockSpec((1, l, tc), lambda j, bi: (bi, 0, j)),
        out_shape=jax.ShapeDtypeStruct((b, l, BRANCH_W), BF16),
        compiler_params=_cparams("arbitrary", "arbitrary"),
        name="hyena_branch",
    )(h, h, h, h, conv_w, conv_w, conv_w, conv_b2, conv_b2, conv_b2,
      filt, filt, skip2, skip2, fwd_bf16, inv_bf16)


def _rms_heads(x, g):
    lo = lax.broadcasted_iota(jnp.int32, x.shape, 1) < HEAD_DIM
    sq = x * x
    s_lo = jnp.sum(jnp.where(lo, sq, 0.0), axis=-1, keepdims=True)
    s_hi = jnp.sum(jnp.where(lo, 0.0, sq), axis=-1, keepdims=True)
    ms = jnp.where(lo, s_lo, s_hi) * (1.0 / HEAD_DIM)
    return x * lax.rsqrt(ms + RMS_EPS) * g


def _rope(x, tab_ref, rows=None):
    sl = slice(None) if rows is None else rows
    up = pltpu.roll(x, LANES - HEAD_DIM // 4, axis=1)
    dn = pltpu.roll(x, HEAD_DIM // 4, axis=1)
    return x * tab_ref[0, sl, :] + up * tab_ref[1, sl, :] + dn * tab_ref[2, sl, :]


def _nt_dot(q, k):
    return lax.dot_general(q, k, (((1,), (1,)), ((), ())), preferred_element_type=F32)


def _softmax_pv(scores, values, sink):
    m = functools.reduce(jnp.maximum, [jnp.max(s, axis=-1, keepdims=True) for s in scores])
    if sink is not None:
        m = jnp.maximum(m, sink)
    ps = [jnp.exp(s - m) for s in scores]
    den = functools.reduce(lambda a, c: a + c, [jnp.sum(p, axis=-1, keepdims=True) for p in ps])
    if sink is not None:
        den = den + jnp.exp(sink - m)
    o = functools.reduce(lambda a, c: a + c, [_dot(p.astype(BF16), v) for p, v in zip(ps, values)])
    return o / den


def _attn_ctx_kernel(sink_ref, qc_ref, gc_ref, qd_ref, gd_ref, kv_ref, qn_ref, kn_ref,
                     oc_ref, od_ref, ck_ref, cv_ref, dk_ref, dv_ref, acc_ref):
    kv = kv_ref[0]
    ck, cv = kv[:, 0:KV_W], kv[:, KV_W:2 * KV_W]
    dk = _rms_heads(kv[:, 2 * KV_W:3 * KV_W], kn_ref[...])
    dv = kv[:, 3 * KV_W:]
    ck_ref[0], cv_ref[0], dk_ref[0], dv_ref[0] = ck, cv, dk, dv

    def branch(q_ref, k, v, use_sink, norm, g_ref, o_ref):
        kb = [k[:, j * HEAD_DIM:(j + 1) * HEAD_DIM].astype(BF16) for j in range(N_KV_HEADS)]
        vb = [v[:, j * HEAD_DIM:(j + 1) * HEAD_DIM].astype(BF16) for j in range(N_KV_HEADS)]
        for c in range(N_HEADS // 2):
            qq = q_ref[0, :, c * LANES:(c + 1) * LANES]
            if norm:
                qq = _rms_heads(qq, qn_ref[...])
            qq = qq * SCALE
            for e in range(2):
                hd = 2 * c + e
                j = hd // GROUP
                s = _nt_dot(qq[:, e * HEAD_DIM:(e + 1) * HEAD_DIM].astype(BF16), kb[j])
                o = _softmax_pv([s], [vb[j]], sink_ref[hd] if use_sink else None)
                acc_ref[:, hd * HEAD_DIM:(hd + 1) * HEAD_DIM] = o
        o_ref[0] = (acc_ref[...] * _silu(g_ref[0])).astype(BF16)

    branch(qc_ref, ck, cv, True, False, gc_ref, oc_ref)
    branch(qd_ref, dk, dv, False, True, gd_ref, od_ref)


def _attention_ctx(h, sink, qn2, kn2):
    b, l, _ = h.shape
    col = lambda blk: pl.BlockSpec((1, l, BRANCH_W), lambda bi: (bi, 0, blk))
    vec = pl.BlockSpec((1, LANES), lambda bi: (0, 0))
    o_spec = pl.BlockSpec((1, l, BRANCH_W), lambda bi: (bi, 0, 0))
    kv_spec = pl.BlockSpec((1, l, KV_W), lambda bi: (bi, 0, 0))
    kv_shape = jax.ShapeDtypeStruct((b, l, KV_W), F32)
    o_shape = jax.ShapeDtypeStruct((b, l, BRANCH_W), BF16)
    return pl.pallas_call(
        _attn_ctx_kernel,
        grid=(b,),
        in_specs=[
            pl.BlockSpec(memory_space=pltpu.SMEM),
            col(COL_CQ), col(COL_CG), col(COL_DQ), col(COL_DG), col(COL_KV), vec, vec,
        ],
        out_specs=[o_spec, o_spec, kv_spec, kv_spec, kv_spec, kv_spec],
        out_shape=[o_shape, o_shape, kv_shape, kv_shape, kv_shape, kv_shape],
        scratch_shapes=[pltpu.VMEM((l, BRANCH_W), F32)],
        compiler_params=_cparams("arbitrary"),
        name="attention_context",
    )(sink, h, h, h, h, h, qn2, kn2)


def _attn_win_kernel(sink_ref, q_ref, g_ref, kv_ref, ck_ref, cv_ref, ropek_ref, ropeq_ref,
                     o_ref, k_scr, v_scr, kc_scr, vc_scr, acc_ref, *, l, tq):
    i = pl.program_id(1)

    @pl.when(i == 0)
    def _():
        k = _rope(kv_ref[0, :, 0:KV_W], ropek_ref)
        v = kv_ref[0, :, KV_W:2 * KV_W]
        kc = ck_ref[0, 0]
        vc = cv_ref[0, 0]
        zeros = jnp.zeros((WINDOW, HEAD_DIM), BF16)
        for j in range(N_KV_HEADS):
            hs = slice(j * HEAD_DIM, (j + 1) * HEAD_DIM)
            for scr, val in ((k_scr, k), (v_scr, v)):
                scr[j, 0:WINDOW, :] = zeros
                scr[j, WINDOW:WINDOW + l, :] = val[:, hs].astype(BF16)
                scr[j, WINDOW + l:, :] = zeros
            kc_scr[j] = kc[:, hs].astype(BF16)
            vc_scr[j] = vc[:, hs].astype(BF16)

    span = tq + 2 * WINDOW
    base = pl.multiple_of(i * tq, tq)
    r = lax.broadcasted_iota(jnp.int32, (tq, span), 0)
    c = lax.broadcasted_iota(jnp.int32, (tq, span), 1)
    kpos = c + (i * tq - WINDOW)
    valid = (c >= r) & (c <= r + 2 * WINDOW) & (kpos >= 0) & (kpos < l)
    for cch in range(N_HEADS // 2):
        qq = _rope(q_ref[0, :, cch * LANES:(cch + 1) * LANES], ropeq_ref)
        qq = qq * SCALE
        for e in range(2):
            hd = 2 * cch + e
            j = hd // GROUP
            qh = qq[:, e * HEAD_DIM:(e + 1) * HEAD_DIM].astype(BF16)
            s_loc = _nt_dot(qh, k_scr[j, pl.ds(base, span), :])
            s_loc = jnp.where(valid, s_loc, NEG_INF)
            s_ctx = _nt_dot(qh, kc_scr[j])
            o = _softmax_pv([s_loc, s_ctx], [v_scr[j, pl.ds(base, span), :], vc_scr[j]], sink_ref[hd])
            acc_ref[:, hd * HEAD_DIM:(hd + 1) * HEAD_DIM] = o
    o_ref[0] = (acc_ref[...] * _silu(g_ref[0])).astype(BF16)


def _attention_window(h, ctx_k, ctx_v, layer, sink, rope_tab):
    b, l, _ = h.shape
    tq = 128
    lc = ctx_k.shape[2]
    ctx_spec = pl.BlockSpec((1, 1, lc, KV_W), lambda bi, i: (bi, layer, 0, 0))
    return pl.pallas_call(
        functools.partial(_attn_win_kernel, l=l, tq=tq),
        grid=(b, l // tq),
        in_specs=[
            pl.BlockSpec(memory_space=pltpu.SMEM),
            pl.BlockSpec((1, tq, BRANCH_W), lambda bi, i: (bi, i, COL_CQ)),
            pl.BlockSpec((1, tq, BRANCH_W), lambda bi, i: (bi, i, COL_CG)),
            pl.BlockSpec((1, l, BRANCH_W), lambda bi, i: (bi, 0, COL_KV)),
            ctx_spec, ctx_spec,
            pl.BlockSpec((3, l, LANES), lambda bi, i: (0, 0, 0)),
            pl.BlockSpec((3, tq, LANES), lambda bi, i: (0, i, 0)),
        ],
        out_specs=pl.BlockSpec((1, tq, BRANCH_W), lambda bi, i: (bi, i, 0)),
        out_shape=jax.ShapeDtypeStruct((b, l, BRANCH_W), BF16),
        scratch_shapes=[
            pltpu.VMEM((N_KV_HEADS, l + 2 * WINDOW, HEAD_DIM), BF16),
            pltpu.VMEM((N_KV_HEADS, l + 2 * WINDOW, HEAD_DIM), BF16),
            pltpu.VMEM((N_KV_HEADS, lc, HEAD_DIM), BF16),
            pltpu.VMEM((N_KV_HEADS, lc, HEAD_DIM), BF16),
            pltpu.VMEM((tq, BRANCH_W), F32),
        ],
        compiler_params=_cparams("arbitrary", "arbitrary"),
        name="attention_window",
    )(sink, h, h, h, ctx_k, ctx_v, rope_tab, rope_tab)


def _attn_full_kernel(q_ref, g_ref, kv_ref, ck_ref, cv_ref, ropek_ref, ropeq_ref, qn_ref, kn_ref,
                      o_ref, k_scr, v_scr, acc_ref, *, l):
    i = pl.program_id(1)

    @pl.when(i == 0)
    def _():
        k = _rope(_rms_heads(kv_ref[0, :, 2 * KV_W:3 * KV_W], kn_ref[...]), ropek_ref)
        v = kv_ref[0, :, 3 * KV_W:]
        kc = ck_ref[0, 0]
        vc = cv_ref[0, 0]
        for j in range(N_KV_HEADS):
            hs = slice(j * HEAD_DIM, (j + 1) * HEAD_DIM)
            k_scr[j, 0:l, :] = k[:, hs].astype(BF16)
            k_scr[j, l:, :] = kc[:, hs].astype(BF16)
            v_scr[j, 0:l, :] = v[:, hs].astype(BF16)
            v_scr[j, l:, :] = vc[:, hs].astype(BF16)

    for cch in range(N_HEADS // 2):
        qq = _rms_heads(q_ref[0, :, cch * LANES:(cch + 1) * LANES], qn_ref[...])
        qq = _rope(qq, ropeq_ref) * SCALE
        for e in range(2):
            hd = 2 * cch + e
            j = hd // GROUP
            s = _nt_dot(qq[:, e * HEAD_DIM:(e + 1) * HEAD_DIM].astype(BF16), k_scr[j])
            acc_ref[:, hd * HEAD_DIM:(hd + 1) * HEAD_DIM] = _softmax_pv([s], [v_scr[j]], None)
    o_ref[0] = (acc_ref[...] * _silu(g_ref[0])).astype(BF16)


def _attention_full(h, ctx_k, ctx_v, layer, rope_tab, qn2, kn2):
    b, l, _ = h.shape
    tq = 256
    lc = ctx_k.shape[2]
    ctx_spec = pl.BlockSpec((1, 1, lc, KV_W), lambda bi, i: (bi, layer, 0, 0))
    vec = pl.BlockSpec((1, LANES), lambda bi, i: (0, 0))
    return pl.pallas_call(
        functools.partial(_attn_full_kernel, l=l),
        grid=(b, l // tq),
        in_specs=[
            pl.BlockSpec((1, tq, BRANCH_W), lambda bi, i: (bi, i, COL_DQ)),
            pl.BlockSpec((1, tq, BRANCH_W), lambda bi, i: (bi, i, COL_DG)),
            pl.BlockSpec((1, l, BRANCH_W), lambda bi, i: (bi, 0, COL_KV)),
            ctx_spec, ctx_spec,
            pl.BlockSpec((3, l, LANES), lambda bi, i: (0, 0, 0)),
            pl.BlockSpec((3, tq, LANES), lambda bi, i: (0, i, 0)),
            vec, vec,
        ],
        out_specs=pl.BlockSpec((1, tq, BRANCH_W), lambda bi, i: (bi, i, 0)),
        out_shape=jax.ShapeDtypeStruct((b, l, BRANCH_W), BF16),
        scratch_shapes=[
            pltpu.VMEM((N_KV_HEADS, l + lc, HEAD_DIM), BF16),
            pltpu.VMEM((N_KV_HEADS, l + lc, HEAD_DIM), BF16),
            pltpu.VMEM((tq, BRANCH_W), F32),
        ],
        compiler_params=_cparams("arbitrary", "arbitrary"),
        name="attention_full",
    )(h, h, h, ctx_k, ctx_v, rope_tab, rope_tab, qn2, kn2)


def _permute_in_weight(w):
    return jnp.concatenate(
        [w[:, :3584], w[:, 3840:4864], w[:, 5120:5632], w[:, 3584:3840], w[:, 4864:5120]],
        axis=1).astype(BF16)


def kernel(x_prompt, x_sample, cache_attn_c_k, cache_attn_c_v, cache_attn_d_k, cache_attn_d_v, c, c_ctx, w_ada, b_ada, w_in, w_fourier, conv_w, conv_b, filt_w1, filt_b1, filt_w2, filt_b2, filt_w3, filt_log_decay, hyena_skip, sink_logit, q_norm, k_norm, w_out, ln_g, ln_b):
    bp, lp, _ = x_prompt.shape
    bs, ls, _ = x_sample.shape
    lc = cache_attn_c_k.shape[2]
    assert lp == lc and bs + 1 <= MOD_ROWS

    cond = jnp.zeros((MOD_ROWS, D_MODEL), F32).at[0].set(c_ctx).at[1:1 + bs].set(c)
    mod = _modulation(cond, w_ada, b_ada)
    ctx_row = lambda bi: 0
    lat_row = lambda bi: bi + 1

    w_cs = _fourier_weights(w_fourier)
    fwd_p, inv_p = (jnp.asarray(t) for t in _conv_dft_tables(lp))
    fwd_s, inv_s = (jnp.asarray(t) for t in _conv_dft_tables(ls))
    filt_p = _hyena_filters(lp, filt_w1, filt_b1, filt_w2, filt_b2, filt_w3, filt_log_decay, fwd_p)
    filt_s = _hyena_filters(ls, filt_w1, filt_b1, filt_w2, filt_b2, filt_w3, filt_log_decay, fwd_s)
    fwd_p, inv_p, fwd_s, inv_s = (t.astype(BF16) for t in (fwd_p, inv_p, fwd_s, inv_s))
    pos_p = jnp.asarray(_fourier_pos_table(lp)).astype(BF16)
    pos_s = jnp.asarray(_fourier_pos_table(ls)).astype(BF16)
    rope_s = jnp.asarray(_rope_tables(ls))

    caches = [t.reshape(bs, DEPTH, lc, KV_W)
              for t in (cache_attn_c_k, cache_attn_c_v, cache_attn_d_k, cache_attn_d_v)]

    fold = max(1, min(bp, 1024 // lp))
    y_p, y_s = x_prompt, x_sample
    new_kv = [[], [], [], []]
    for layer in range(DEPTH):
        w_in_l = _permute_in_weight(w_in[layer])
        w_out_l = w_out[layer].astype(BF16).reshape(4, BRANCH_W, D_MODEL)
        qn2 = jnp.tile(q_norm[layer], LANES // HEAD_DIM).reshape(1, LANES)
        kn2 = jnp.tile(k_norm[layer], LANES // HEAD_DIM).reshape(1, LANES)
        sink = sink_logit[layer]

        xf = y_p.reshape(bp // fold, fold * lp, D_MODEL)
        h = _in_projection(xf, mod[layer], ctx_row, w_in_l).reshape(bp, lp, IN_WIDTH)
        ya = _fourier_branch(h, w_cs[layer], pos_p)
        yb = _hyena_branch(h, conv_w[layer], conv_b[layer], filt_p, layer, hyena_skip[layer], fwd_p, inv_p)
        oc, od, ck, cv, dk, dv = _attention_ctx(h, sink, qn2, kn2)
        for lst, t in zip(new_kv, (ck, cv, dk, dv)):
            lst.append(t.reshape(bp, lp, N_KV_HEADS, HEAD_DIM))
        parts = [t.reshape(bp // fold, fold * lp, BRANCH_W) for t in (ya, yb, oc, od)]
        y_p = _out_projection(parts, xf, mod[layer], ctx_row, w_out_l, ln_g[layer], ln_b[layer])
        y_p = y_p.reshape(bp, lp, D_MODEL)

        h = _in_projection(y_s, mod[layer], lat_row, w_in_l)
        ya = _fourier_branch(h, w_cs[layer], pos_s)
        yb = _hyena_branch(h, conv_w[layer], conv_b[layer], filt_s, layer, hyena_skip[layer], fwd_s, inv_s)
        oc = _attention_window(h, caches[0], caches[1], layer, sink, rope_s)
        od = _attention_full(h, caches[2], caches[3], layer, rope_s, qn2, kn2)
        y_s = _out_projection([ya, yb, oc, od], y_s, mod[layer], lat_row, w_out_l, ln_g[layer], ln_b[layer])

    return (y_p, y_s) + tuple(jnp.stack(lst, axis=1) for lst in new_kv)
```

```python
import functools

import numpy as np
import jax
import jax.numpy as jnp
from jax import lax
from jax.experimental import pallas as pl
from jax.experimental.pallas import tpu as pltpu

D_MODEL = 2048
DEPTH = 2
GRID_W = 64
HEAD_DIM = 64
BRANCH_W = 512
FOURIER_GROUPS = 8
FOURIER_GROUP_W = BRANCH_W // FOURIER_GROUPS
FILTER_FREQS = 16
FILTER_FEATS = 1 + 2 * FILTER_FREQS
FILTER_HIDDEN = 64
DECAY_SHIFT = 0.05
N_HEADS = 8
N_KV_HEADS = 2
GROUP = N_HEADS // N_KV_HEADS
KV_W = N_KV_HEADS * HEAD_DIM
WINDOW = 128
ROPE_THETA = 10000.0
ALPHA = (2 * DEPTH) ** 0.25
LN_EPS = 1e-5
RMS_EPS = 1e-6
NEG_INF = -1e30
IN_WIDTH = 5632
SCALE = HEAD_DIM ** -0.5

F32 = jnp.float32
BF16 = jnp.bfloat16

VMEM_LIMIT_BYTES = 56 * 1024 * 1024
LANES = 128

COL_A_IN, COL_A_G, COL_B_V, COL_B_X1, COL_B_X2, COL_B_G = 0, 1, 2, 3, 4, 5
COL_CQ, COL_CG, COL_DQ, COL_DG, COL_KV = 6, 7, 8, 9, 10
MOD_ROWS = 16


def _cparams(*sem):
    return pltpu.CompilerParams(dimension_semantics=sem, vmem_limit_bytes=VMEM_LIMIT_BYTES)


def _silu(x):
    return x * jax.nn.sigmoid(x)


def _dot(a, b):
    return jnp.dot(a, b, preferred_element_type=F32)


def _dot_exact(a, b):
    return jnp.dot(a, b, preferred_element_type=F32, precision=lax.Precision.HIGHEST)


@functools.lru_cache(maxsize=None)
def _fourier_pos_table(l):
    k = np.arange(l, dtype=np.int64)
    ang = 2.0 * np.pi * ((k[:, None] * k[None, :]) % l) / l
    return (np.concatenate([np.cos(ang), -np.sin(ang)], axis=1) / np.sqrt(l)).astype(np.float32)


@functools.lru_cache(maxsize=None)
def _fourier_group_tables():
    w = FOURIER_GROUP_W
    k = np.arange(w, dtype=np.int64)
    ang = 2.0 * np.pi * ((k[:, None] * k[None, :]) % w) / w
    eye = np.eye(FOURIER_GROUPS)
    c = np.kron(eye, np.cos(ang)) / np.sqrt(w)
    s = np.kron(eye, np.sin(ang)) / np.sqrt(w)
    return np.stack([c, s]).astype(np.float32)


@functools.lru_cache(maxsize=None)
def _conv_dft_tables(l):
    n = 2 * l
    k = np.arange(l, dtype=np.int64)
    s = np.arange(l, dtype=np.int64)
    ang = 2.0 * np.pi * ((k[:, None] * s[None, :]) % n) / n
    fr = np.cos(ang)
    fi = -np.sin(ang)
    fi[0, :] = np.where(s % 2 == 0, 1.0, -1.0)
    fwd = np.concatenate([fr, fi], axis=0)
    t = np.arange(l, dtype=np.int64) + l // 2
    ang_i = 2.0 * np.pi * ((t[:, None] * k[None, :]) % n) / n
    wk = np.where(k == 0, 1.0, 2.0) / n
    gr = np.cos(ang_i) * wk[None, :]
    gi = -np.sin(ang_i) * wk[None, :]
    gi[:, 0] = np.where(t % 2 == 0, 1.0, -1.0) / n
    inv = np.concatenate([gr, gi], axis=1)
    return fwd.astype(np.float32), inv.astype(np.float32)


@functools.lru_cache(maxsize=None)
def _filter_feats(l):
    t = np.arange(l, dtype=np.float64) / l
    f = np.arange(1, FILTER_FREQS + 1, dtype=np.float64)
    ang = 2.0 * np.pi * t[:, None] * f[None, :]
    feats = np.concatenate([t[:, None], np.cos(ang), np.sin(ang)], axis=1)
    out = np.zeros((l, FILTER_HIDDEN), np.float64)
    out[:, :FILTER_FEATS] = feats
    return out.astype(np.float32)


@functools.lru_cache(maxsize=None)
def _rope_tables(l):
    t = np.arange(l)
    row = (t // GRID_W).astype(np.float64)
    col = (t % GRID_W).astype(np.float64)
    half = HEAD_DIM // 2
    inv = ROPE_THETA ** (-np.arange(0, half, 2, dtype=np.float64) / half)
    ar = row[:, None] * inv
    ac = col[:, None] * inv
    cos_h = np.concatenate([np.cos(ar), np.cos(ar), np.cos(ac), np.cos(ac)], axis=1)
    zero = np.zeros_like(ar)
    sin_up = np.concatenate([-np.sin(ar), zero, -np.sin(ac), zero], axis=1)
    sin_dn = np.concatenate([zero, np.sin(ar), zero, np.sin(ac)], axis=1)
    tabs = [np.tile(x, (1, LANES // HEAD_DIM)) for x in (cos_h, sin_up, sin_dn)]
    return np.stack(tabs).astype(np.float32)


def _mod_kernel(cond_ref, w_ref, b_ref, o_ref):
    s = _silu(cond_ref[...])
    o_ref[0] = _dot(s.astype(BF16), w_ref[0].astype(BF16)) + b_ref[0]


def _modulation(cond, w_ada, b_ada):
    tn = 512
    n = 3 * D_MODEL
    out = pl.pallas_call(
        _mod_kernel,
        grid=(DEPTH, n // tn),
        in_specs=[
            pl.BlockSpec((MOD_ROWS, D_MODEL), lambda l, j: (0, 0)),
            pl.BlockSpec((1, D_MODEL, tn), lambda l, j: (l, 0, j)),
            pl.BlockSpec((1, 1, tn), lambda l, j: (l, 0, j)),
        ],
        out_specs=pl.BlockSpec((1, MOD_ROWS, tn), lambda l, j: (l, 0, j)),
        out_shape=jax.ShapeDtypeStruct((DEPTH, MOD_ROWS, n), F32),
        compiler_params=_cparams("arbitrary", "arbitrary"),
        name="modulation",
    )(cond, w_ada, b_ada.reshape(DEPTH, 1, n))
    return out.reshape(DEPTH, MOD_ROWS, 3, D_MODEL)


def _layer_norm_rows(x):
    mu = jnp.mean(x, axis=-1, keepdims=True)
    xc = x - mu
    var = jnp.mean(xc * xc, axis=-1, keepdims=True)
    return xc * lax.rsqrt(var + LN_EPS)


def _inproj_kernel(x_ref, mod_ref, w_ref, o_ref, hm_ref):
    @pl.when(pl.program_id(2) == 0)
    def _():
        y = _layer_norm_rows(x_ref[0])
        hm = y * (1.0 + mod_ref[0, 1:2, :]) + mod_ref[0, 0:1, :]
        hm_ref[...] = hm.astype(BF16)

    o_ref[0] = _dot(hm_ref[...], w_ref[0])


def _in_projection(x, mod, mod_row, w_tiles):
    b, l, _ = x.shape
    tl = min(l, 1024)
    tn = w_tiles.shape[2]
    return pl.pallas_call(
        _inproj_kernel,
        grid=(b, l // tl, IN_WIDTH // tn),
        in_specs=[
            pl.BlockSpec((1, tl, D_MODEL), lambda bi, i, j: (bi, i, 0)),
            pl.BlockSpec((1, 3, D_MODEL), lambda bi, i, j: (mod_row(bi), 0, 0)),
            pl.BlockSpec((1, D_MODEL, tn), lambda bi, i, j: (j, 0, 0)),
        ],
        out_specs=pl.BlockSpec((1, tl, tn), lambda bi, i, j: (bi, i, j)),
        out_shape=jax.ShapeDtypeStruct((b, l, IN_WIDTH), F32),
        scratch_shapes=[pltpu.VMEM((tl, D_MODEL), BF16)],
        compiler_params=_cparams("arbitrary", "arbitrary", "arbitrary"),
        name="in_projection",
    )(x, mod, w_tiles)


def _outproj_kernel(ya_ref, yb_ref, oc_ref, od_ref, x_ref, mod_ref, w_ref, g_ref, b_ref, o_ref):
    y = _dot(ya_ref[0], w_ref[0])
    y += _dot(yb_ref[0], w_ref[1])
    y += _dot(oc_ref[0], w_ref[2])
    y += _dot(od_ref[0], w_ref[3])
    z = ALPHA * x_ref[0] + mod_ref[0, 2:3, :] * y
    o_ref[0] = _layer_norm_rows(z) * g_ref[...] + b_ref[...]


def _out_projection(parts, x, mod, mod_row, w_out_bf16, ln_g, ln_b):
    b, l, _ = x.shape
    tl = min(l, 512)
    part_spec = pl.BlockSpec((1, tl, BRANCH_W), lambda bi, i: (bi, i, 0))
    row_spec = pl.BlockSpec((1, tl, D_MODEL), lambda bi, i: (bi, i, 0))
    vec_spec = pl.BlockSpec((1, D_MODEL), lambda bi, i: (0, 0))
    return pl.pallas_call(
        _outproj_kernel,
        grid=(b, l // tl),
        in_specs=[
            part_spec, part_spec, part_spec, part_spec, row_spec,
            pl.BlockSpec((1, 3, D_MODEL), lambda bi, i: (mod_row(bi), 0, 0)),
            pl.BlockSpec((4, BRANCH_W, D_MODEL), lambda bi, i: (0, 0, 0)),
            vec_spec, vec_spec,
        ],
        out_specs=row_spec,
        out_shape=jax.ShapeDtypeStruct((b, l, D_MODEL), F32),
        compiler_params=_cparams("arbitrary", "arbitrary"),
        name="out_projection",
    )(*parts, x, mod, w_out_bf16, ln_g.reshape(1, D_MODEL), ln_b.reshape(1, D_MODEL))


def _fourier_weight_kernel(t_ref, w_ref, o_ref):
    w = w_ref[0]
    o_ref[0, :, :BRANCH_W] = _dot_exact(t_ref[0], w).astype(BF16)
    o_ref[0, :, BRANCH_W:] = _dot_exact(t_ref[1], w).astype(BF16)


def _fourier_weights(w_fourier):
    return pl.pallas_call(
        _fourier_weight_kernel,
        grid=(DEPTH,),
        in_specs=[
            pl.BlockSpec((2, BRANCH_W, BRANCH_W), lambda l: (0, 0, 0)),
            pl.BlockSpec((1, BRANCH_W, BRANCH_W), lambda l: (l, 0, 0)),
        ],
        out_specs=pl.BlockSpec((1, BRANCH_W, 2 * BRANCH_W), lambda l: (l, 0, 0)),
        out_shape=jax.ShapeDtypeStruct((DEPTH, BRANCH_W, 2 * BRANCH_W), BF16),
        compiler_params=_cparams("arbitrary"),
        name="fourier_weights",
    )(jnp.asarray(_fourier_group_tables()), w_fourier)


def _fourier_kernel(a_ref, g_ref, w_ref, t_ref, o_ref):
    pq = _dot(a_ref[0].astype(BF16), w_ref[...])
    pq = jnp.concatenate([pq[:, :BRANCH_W], pq[:, BRANCH_W:]], axis=0).astype(BF16)
    y = _dot(t_ref[...], pq)
    o_ref[0] = (y * _silu(g_ref[0])).astype(BF16)


def _fourier_branch(h, w_cs, pos_table_bf16):
    b, l, _ = h.shape
    return pl.pallas_call(
        _fourier_kernel,
        grid=(b,),
        in_specs=[
            pl.BlockSpec((1, l, BRANCH_W), lambda bi: (bi, 0, COL_A_IN)),
            pl.BlockSpec((1, l, BRANCH_W), lambda bi: (bi, 0, COL_A_G)),
            pl.BlockSpec((BRANCH_W, 2 * BRANCH_W), lambda bi: (0, 0)),
            pl.BlockSpec((l, 2 * l), lambda bi: (0, 0)),
        ],
        out_specs=pl.BlockSpec((1, l, BRANCH_W), lambda bi: (bi, 0, 0)),
        out_shape=jax.ShapeDtypeStruct((b, l, BRANCH_W), BF16),
        compiler_params=_cparams("arbitrary"),
        name="fourier_branch",
    )(h, h, w_cs, pos_table_bf16)


def _hyena_filter_kernel(feat_ref, w1_ref, b1_ref, w2_ref, b2_ref, w3_ref, dec_ref, fhi_ref, flo_ref, o_ref):
    l = feat_ref.shape[0]
    h = jnp.sin(_dot_exact(feat_ref[...], w1_ref[0]) + b1_ref[0])
    h = jnp.sin(_dot_exact(h, w2_ref[0]) + b2_ref[0])
    h = _dot_exact(h, w3_ref[0])
    t = lax.broadcasted_iota(jnp.int32, h.shape, 0)
    dist = jnp.abs(t - l // 2).astype(F32) * (1.0 / (l / 2))
    window = jnp.exp(-dist * jnp.exp(dec_ref[0])) + DECAY_SHIFT
    h = h * window
    h = h * lax.rsqrt(jnp.sum(h * h, axis=0, keepdims=True) + RMS_EPS)
    h_hi = h.astype(BF16)
    h_lo = (h - h_hi.astype(F32)).astype(BF16)
    spec = _dot(fhi_ref[...], h_hi) + (_dot(fhi_ref[...], h_lo) + _dot(flo_ref[...], h_hi))
    hr = spec[:l]
    hi = spec[l:]
    first = t == 0
    o_ref[0, 0] = hr
    o_ref[0, 1] = jnp.where(first, 0.0, hi)
    o_ref[0, 2] = jnp.where(first, hi, hr)


def _hyena_filters(l, fw1, fb1, fw2, fb2, fw3, log_decay, fwd_hi, fwd_lo):
    c = 2 * BRANCH_W
    tc = 256
    w1 = jnp.pad(fw1, ((0, 0), (0, FILTER_HIDDEN - FILTER_FEATS), (0, 0)))
    vec = lambda n: pl.BlockSpec((1, 1, n), lambda d, j: (d, 0, 0))
    mat = lambda m, n: pl.BlockSpec((1, m, n), lambda d, j: (d, 0, 0))
    return pl.pallas_call(
        _hyena_filter_kernel,
        grid=(DEPTH, c // tc),
        in_specs=[
            pl.BlockSpec((l, FILTER_HIDDEN), lambda d, j: (0, 0)),
            mat(FILTER_HIDDEN, FILTER_HIDDEN), vec(FILTER_HIDDEN),
            mat(FILTER_HIDDEN, FILTER_HIDDEN), vec(FILTER_HIDDEN),
            pl.BlockSpec((1, FILTER_HIDDEN, tc), lambda d, j: (d, 0, j)),
            pl.BlockSpec((1, 1, tc), lambda d, j: (d, 0, j)),
            pl.BlockSpec((2 * l, l), lambda d, j: (0, 0)),
            pl.BlockSpec((2 * l, l), lambda d, j: (0, 0)),
        ],
        out_specs=pl.BlockSpec((1, 3, l, tc), lambda d, j: (d, 0, 0, j)),
        out_shape=jax.ShapeDtypeStruct((DEPTH, 3, l, c), F32),
        compiler_params=_cparams("arbitrary", "arbitrary"),
        name="hyena_filters",
    )(jnp.asarray(_filter_feats(l)), w1, fb1.reshape(DEPTH, 1, -1), fw2, fb2.reshape(DEPTH, 1, -1),
      fw3, log_decay.reshape(DEPTH, 1, c), fwd_hi, fwd_lo)


def _short_conv(u, w, b):
    l = u.shape[0]
    t = lax.broadcasted_iota(jnp.int32, u.shape, 0)
    prev = jnp.where(t == 0, 0.0, pltpu.roll(u, 1, axis=0))
    nxt = jnp.where(t == l - 1, 0.0, pltpu.roll(u, l - 1, axis=0))
    return prev * w[0:1] + u * w[1:2] + nxt * w[2:3] + b


def _long_conv(u, filt_ref, cols, skip, fwd_ref, inv_ref):
    l = u.shape[0]
    spec = _dot(fwd_ref[...], u.astype(BF16))
    ur, ui = spec[:l], spec[l:]
    hr, hi, hn = filt_ref[0, 0, :, cols], filt_ref[0, 1, :, cols], filt_ref[0, 2, :, cols]
    zr = ur * hr - ui * hi
    zi = ur * hi + ui * hn
    z = jnp.concatenate([zr, zi], axis=0).astype(BF16)
    return _dot(inv_ref[...], z) + u * skip


HYENA_CHAIN_W = 256


def _hyena_kernel(v_ref, x1_ref, x2_ref, g_ref, cwv_ref, cwx1_ref, cwx2_ref, cbv_ref, cbx1_ref,
                  cbx2_ref, f1_ref, f2_ref, s1_ref, s2_ref, fwd_ref, inv_ref, o_ref):
    for k in range(o_ref.shape[2] // HYENA_CHAIN_W):
        cs = slice(k * HYENA_CHAIN_W, (k + 1) * HYENA_CHAIN_W)
        v = _short_conv(v_ref[0, :, cs], cwv_ref[:, cs], cbv_ref[:, cs])
        x1 = _short_conv(x1_ref[0, :, cs], cwx1_ref[:, cs], cbx1_ref[:, cs])
        z = x1 * _long_conv(v, f1_ref, cs, s1_ref[:, cs], fwd_ref, inv_ref)
        x2 = _short_conv(x2_ref[0, :, cs], cwx2_ref[:, cs], cbx2_ref[:, cs])
        y = x2 * _long_conv(z, f2_ref, cs, s2_ref[:, cs], fwd_ref, inv_ref)
        o_ref[0, :, cs] = (y * _silu(g_ref[0, :, cs])).astype(BF16)


def _hyena_branch(h, conv_w, conv_b, filt, layer, skip, fwd_bf16, inv_bf16):
    b, l, _ = h.shape
    w = BRANCH_W
    once = pl.Buffered(1)
    col = lambda blk: pl.BlockSpec((1, l, w), lambda bi: (bi, 0, blk))
    cw = lambda part: pl.BlockSpec((3, w), lambda bi: (0, part), pipeline_mode=once)
    cb = lambda part: pl.BlockSpec((1, w), lambda bi: (0, part), pipeline_mode=once)
    flt = lambda which: pl.BlockSpec((1, 3, l, w), lambda bi: (layer, 0, 0, which), pipeline_mode=once)
    skp = lambda which: pl.BlockSpec((1, w), lambda bi: (0, which), pipeline_mode=once)
    conv_b2 = conv_b.reshape(1, -1)
    skip2 = skip.reshape(1, -1)
    return pl.pallas_call(
        _hyena_kernel,
        grid=(b,),
        in_specs=[
            col(COL_B_V), col(COL_B_X1), col(COL_B_X2), col(COL_B_G),
            cw(0), cw(1), cw(2), cb(0), cb(1), cb(2),
            flt(0), flt(1), skp(0), skp(1),
            pl.BlockSpec((2 * l, l), lambda bi: (0, 0), pipeline_mode=once),
            pl.BlockSpec((l, 2 * l), lambda bi: (0, 0), pipeline_mode=once),
        ],
        out_specs=pl.BlockSpec((1, l, w), lambda bi: (bi, 0, 0)),
        out_shape=jax.ShapeDtypeStruct((b, l, w), BF16),
        compiler_params=_cparams("arbitrary"),
        name="hyena_branch",
    )(h, h, h, h, conv_w, conv_w, conv_w, conv_b2, conv_b2, conv_b2,
      filt, filt, skip2, skip2, fwd_bf16, inv_bf16)


def _lane_lo(shape):
    return lax.broadcasted_iota(jnp.int32, shape, len(shape) - 1) < HEAD_DIM


def _rms_heads(x, g):
    lo = _lane_lo(x.shape)
    sq = x * x
    s_lo = jnp.sum(jnp.where(lo, sq, 0.0), axis=-1, keepdims=True)
    s_hi = jnp.sum(jnp.where(lo, 0.0, sq), axis=-1, keepdims=True)
    ms = jnp.where(lo, s_lo, s_hi) * (1.0 / HEAD_DIM)
    return x * lax.rsqrt(ms + RMS_EPS) * g


def _rope(x, tab_ref):
    up = pltpu.roll(x, LANES - HEAD_DIM // 4, axis=1)
    dn = pltpu.roll(x, HEAD_DIM // 4, axis=1)
    return x * tab_ref[0] + up * tab_ref[1] + dn * tab_ref[2]


def _dup_heads(x):
    lo = _lane_lo(x.shape)
    sw = pltpu.roll(x, HEAD_DIM, axis=1)
    return jnp.where(lo, x, sw).astype(BF16), jnp.where(lo, sw, x).astype(BF16)


def _stack_group(chunks):
    parts = []
    for x in chunks:
        lo = _lane_lo(x.shape)
        parts.append(jnp.where(lo, x, 0.0).astype(BF16))
        parts.append(jnp.where(lo, 0.0, x).astype(BF16))
    return jnp.concatenate(parts, axis=0)


def _nt_dot(q, k):
    return lax.dot_general(q, k, (((1,), (1,)), ((), ())), preferred_element_type=F32)


def _group_attention(q4, keys, values, sinks, masks, tq):
    scores = [_nt_dot(q4, k) for k in keys]
    probs = [[] for _ in keys]
    inv = []
    for r in range(GROUP):
        rows = slice(r * tq, (r + 1) * tq)
        s_r = [s[rows] if mk is None else jnp.where(mk, s[rows], NEG_INF) for s, mk in zip(scores, masks)]
        m = functools.reduce(jnp.maximum, [jnp.max(s, axis=-1, keepdims=True) for s in s_r])
        if sinks is not None:
            m = jnp.maximum(m, sinks[r])
        p_r = [jnp.exp(s - m) for s in s_r]
        den = functools.reduce(lambda a, c: a + c, [jnp.sum(p, axis=-1, keepdims=True) for p in p_r])
        if sinks is not None:
            den = den + jnp.exp(sinks[r] - m)
        inv.append(1.0 / den)
        for lst, p in zip(probs, p_r):
            lst.append(p.astype(BF16))
    o = functools.reduce(lambda a, c: a + c,
                         [_dot(jnp.concatenate(p, axis=0), v) for p, v in zip(probs, values)])
    o_r = [o[r * tq:(r + 1) * tq] * inv[r] for r in range(GROUP)]
    lo = _lane_lo(o_r[0].shape)
    return [jnp.where(lo, o_r[0], o_r[1]), jnp.where(lo, o_r[2], o_r[3])]


def _attend_groups(prep_q, g_ref, o_ref, kv_of_group, sinks_of_group, masks, tq):
    for j in range(N_KV_HEADS):
        q4 = _stack_group([prep_q(2 * j + c) for c in range(2)])
        keys, values = kv_of_group(j)
        outs = _group_attention(q4, keys, values, sinks_of_group(j), masks, tq)
        for c in range(2):
            cols = slice((2 * j + c) * LANES, (2 * j + c + 1) * LANES)
            o_ref[0, :, cols] = (outs[c] * _silu(g_ref[0, :, cols])).astype(BF16)


def _attn_ctx_kernel(sink_ref, qc_ref, gc_ref, qd_ref, gd_ref, kv_ref, qn_ref, kn_ref,
                     oc_ref, od_ref, ck_ref, cv_ref, dk_ref, dv_ref):
    tq = qc_ref.shape[1]
    kv = kv_ref[0]
    ck, cv = kv[:, 0:KV_W], kv[:, KV_W:2 * KV_W]
    dk = _rms_heads(kv[:, 2 * KV_W:3 * KV_W], kn_ref[...])
    dv = kv[:, 3 * KV_W:]
    ck_ref[0], cv_ref[0], dk_ref[0], dv_ref[0] = ck, cv, dk, dv

    ckd, cvd = _dup_heads(ck), _dup_heads(cv)
    _attend_groups(lambda c: qc_ref[0, :, c * LANES:(c + 1) * LANES] * SCALE, gc_ref, oc_ref,
                   lambda j: ([ckd[j]], [cvd[j]]),
                   lambda j: [sink_ref[GROUP * j + r] for r in range(GROUP)], [None], tq)
    dkd, dvd = _dup_heads(dk), _dup_heads(dv)
    _attend_groups(lambda c: _rms_heads(qd_ref[0, :, c * LANES:(c + 1) * LANES], qn_ref[...]) * SCALE,
                   gd_ref, od_ref, lambda j: ([dkd[j]], [dvd[j]]), lambda j: None, [None], tq)


def _attention_ctx(h, sink, qn2, kn2):
    b, l, _ = h.shape
    col = lambda blk: pl.BlockSpec((1, l, BRANCH_W), lambda bi: (bi, 0, blk))
    vec = pl.BlockSpec((1, LANES), lambda bi: (0, 0))
    o_spec = pl.BlockSpec((1, l, BRANCH_W), lambda bi: (bi, 0, 0))
    kv_spec = pl.BlockSpec((1, l, KV_W), lambda bi: (bi, 0, 0))
    kv_shape = jax.ShapeDtypeStruct((b, l, KV_W), F32)
    o_shape = jax.ShapeDtypeStruct((b, l, BRANCH_W), BF16)
    return pl.pallas_call(
        _attn_ctx_kernel,
        grid=(b,),
        in_specs=[
            pl.BlockSpec(memory_space=pltpu.SMEM),
            col(COL_CQ), col(COL_CG), col(COL_DQ), col(COL_DG), col(COL_KV), vec, vec,
        ],
        out_specs=[o_spec, o_spec, kv_spec, kv_spec, kv_spec, kv_spec],
        out_shape=[o_shape, o_shape, kv_shape, kv_shape, kv_shape, kv_shape],
        compiler_params=_cparams("arbitrary"),
        name="attention_context",
    )(sink, h, h, h, h, h, qn2, kn2)


def _attn_win_kernel(sink_ref, q_ref, g_ref, kv_ref, ck_ref, cv_ref, ropek_ref, ropeq_ref,
                     o_ref, k_scr, v_scr, kc_scr, vc_scr, *, l, tq):
    i = pl.program_id(1)

    @pl.when(i == 0)
    def _():
        zeros = jnp.zeros((WINDOW, LANES), BF16)
        for scr, val in ((k_scr, _dup_heads(_rope(kv_ref[0, :, 0:KV_W], ropek_ref))),
                         (v_scr, _dup_heads(kv_ref[0, :, KV_W:2 * KV_W]))):
            for j in range(N_KV_HEADS):
                scr[j, 0:WINDOW, :] = zeros
                scr[j, WINDOW:WINDOW + l, :] = val[j]
                scr[j, WINDOW + l:, :] = zeros
        for scr, val in ((kc_scr, _dup_heads(ck_ref[0, 0])), (vc_scr, _dup_heads(cv_ref[0, 0]))):
            for j in range(N_KV_HEADS):
                scr[j] = val[j]

    span = tq + 2 * WINDOW
    base = pl.multiple_of(i * tq, tq)
    r = lax.broadcasted_iota(jnp.int32, (tq, span), 0)
    c = lax.broadcasted_iota(jnp.int32, (tq, span), 1)
    kpos = c + (i * tq - WINDOW)
    valid = (c >= r) & (c <= r + 2 * WINDOW) & (kpos >= 0) & (kpos < l)
    _attend_groups(lambda ch: _rope(q_ref[0, :, ch * LANES:(ch + 1) * LANES], ropeq_ref) * SCALE,
                   g_ref, o_ref,
                   lambda j: ([k_scr[j, pl.ds(base, span), :], kc_scr[j]],
                              [v_scr[j, pl.ds(base, span), :], vc_scr[j]]),
                   lambda j: [sink_ref[GROUP * j + r_] for r_ in range(GROUP)], [valid, None], tq)


def _attention_window(h, ctx_k, ctx_v, layer, sink, rope_tab):
    b, l, _ = h.shape
    tq = 128
    lc = ctx_k.shape[2]
    ctx_spec = pl.BlockSpec((1, 1, lc, KV_W), lambda bi, i: (bi, layer, 0, 0))
    return pl.pallas_call(
        functools.partial(_attn_win_kernel, l=l, tq=tq),
        grid=(b, l // tq),
        in_specs=[
            pl.BlockSpec(memory_space=pltpu.SMEM),
            pl.BlockSpec((1, tq, BRANCH_W), lambda bi, i: (bi, i, COL_CQ)),
            pl.BlockSpec((1, tq, BRANCH_W), lambda bi, i: (bi, i, COL_CG)),
            pl.BlockSpec((1, l, BRANCH_W), lambda bi, i: (bi, 0, COL_KV)),
            ctx_spec, ctx_spec,
            pl.BlockSpec((3, l, LANES), lambda bi, i: (0, 0, 0)),
            pl.BlockSpec((3, tq, LANES), lambda bi, i: (0, i, 0)),
        ],
        out_specs=pl.BlockSpec((1, tq, BRANCH_W), lambda bi, i: (bi, i, 0)),
        out_shape=jax.ShapeDtypeStruct((b, l, BRANCH_W), BF16),
        scratch_shapes=[
            pltpu.VMEM((N_KV_HEADS, l + 2 * WINDOW, LANES), BF16),
            pltpu.VMEM((N_KV_HEADS, l + 2 * WINDOW, LANES), BF16),
            pltpu.VMEM((N_KV_HEADS, lc, LANES), BF16),
            pltpu.VMEM((N_KV_HEADS, lc, LANES), BF16),
        ],
        compiler_params=_cparams("arbitrary", "arbitrary"),
        name="attention_window",
    )(sink, h, h, h, ctx_k, ctx_v, rope_tab, rope_tab)


def _attn_full_kernel(q_ref, g_ref, kv_ref, ck_ref, cv_ref, ropek_ref, ropeq_ref, qn_ref, kn_ref,
                      o_ref, k_scr, v_scr, *, l, tq):
    i = pl.program_id(1)

    @pl.when(i == 0)
    def _():
        k = _rope(_rms_heads(kv_ref[0, :, 2 * KV_W:3 * KV_W], kn_ref[...]), ropek_ref)
        for scr, loc, ctx in ((k_scr, _dup_heads(k), _dup_heads(ck_ref[0, 0])),
                              (v_scr, _dup_heads(kv_ref[0, :, 3 * KV_W:]), _dup_heads(cv_ref[0, 0]))):
            for j in range(N_KV_HEADS):
                scr[j, 0:l, :] = loc[j]
                scr[j, l:, :] = ctx[j]

    def prep_q(ch):
        qq = _rms_heads(q_ref[0, :, ch * LANES:(ch + 1) * LANES], qn_ref[...])
        return _rope(qq, ropeq_ref) * SCALE

    _attend_groups(prep_q, g_ref, o_ref, lambda j: ([k_scr[j]], [v_scr[j]]), lambda j: None, [None], tq)


def _attention_full(h, ctx_k, ctx_v, layer, rope_tab, qn2, kn2):
    b, l, _ = h.shape
    tq = 256
    lc = ctx_k.shape[2]
    ctx_spec = pl.BlockSpec((1, 1, lc, KV_W), lambda bi, i: (bi, layer, 0, 0))
    vec = pl.BlockSpec((1, LANES), lambda bi, i: (0, 0))
    return pl.pallas_call(
        functools.partial(_attn_full_kernel, l=l, tq=tq),
        grid=(b, l // tq),
        in_specs=[
            pl.BlockSpec((1, tq, BRANCH_W), lambda bi, i: (bi, i, COL_DQ)),
            pl.BlockSpec((1, tq, BRANCH_W), lambda bi, i: (bi, i, COL_DG)),
            pl.BlockSpec((1, l, BRANCH_W), lambda bi, i: (bi, 0, COL_KV)),
            ctx_spec, ctx_spec,
            pl.BlockSpec((3, l, LANES), lambda bi, i: (0, 0, 0)),
            pl.BlockSpec((3, tq, LANES), lambda bi, i: (0, i, 0)),
            vec, vec,
        ],
        out_specs=pl.BlockSpec((1, tq, BRANCH_W), lambda bi, i: (bi, i, 0)),
        out_shape=jax.ShapeDtypeStruct((b, l, BRANCH_W), BF16),
        scratch_shapes=[
            pltpu.VMEM((N_KV_HEADS, l + lc, LANES), BF16),
            pltpu.VMEM((N_KV_HEADS, l + lc, LANES), BF16),
        ],
        compiler_params=_cparams("arbitrary", "arbitrary"),
        name="attention_full",
    )(h, h, h, ctx_k, ctx_v, rope_tab, rope_tab, qn2, kn2)


def _permute_in_weight(w, tn):
    wp = jnp.concatenate(
        [w[:, :3584], w[:, 3840:4864], w[:, 5120:5632], w[:, 3584:3840], w[:, 4864:5120]],
        axis=1).astype(BF16)
    return wp.reshape(D_MODEL, IN_WIDTH // tn, tn).transpose(1, 0, 2)


def kernel(x_prompt, x_sample, cache_attn_c_k, cache_attn_c_v, cache_attn_d_k, cache_attn_d_v, c, c_ctx, w_ada, b_ada, w_in, w_fourier, conv_w, conv_b, filt_w1, filt_b1, filt_w2, filt_b2, filt_w3, filt_log_decay, hyena_skip, sink_logit, q_norm, k_norm, w_out, ln_g, ln_b):
    bp, lp, _ = x_prompt.shape
    bs, ls, _ = x_sample.shape
    lc = cache_attn_c_k.shape[2]
    assert lp == lc and bs + 1 <= MOD_ROWS

    cond = jnp.zeros((MOD_ROWS, D_MODEL), F32).at[0].set(c_ctx).at[1:1 + bs].set(c)
    mod = _modulation(cond, w_ada, b_ada)
    ctx_row = lambda bi: 0
    lat_row = lambda bi: bi + 1

    w_cs = _fourier_weights(w_fourier)
    def conv_tables(l):
        fwd, inv = (jnp.asarray(t) for t in _conv_dft_tables(l))
        fwd_hi = fwd.astype(BF16)
        fwd_lo = (fwd - fwd_hi.astype(F32)).astype(BF16)
        return fwd_hi, fwd_lo, inv.astype(BF16)

    fwd_p, fwd_lo_p, inv_p = conv_tables(lp)
    fwd_s, fwd_lo_s, inv_s = conv_tables(ls)
    filt_p = _hyena_filters(lp, filt_w1, filt_b1, filt_w2, filt_b2, filt_w3, filt_log_decay, fwd_p, fwd_lo_p)
    filt_s = _hyena_filters(ls, filt_w1, filt_b1, filt_w2, filt_b2, filt_w3, filt_log_decay, fwd_s, fwd_lo_s)
    pos_p = jnp.asarray(_fourier_pos_table(lp)).astype(BF16)
    pos_s = jnp.asarray(_fourier_pos_table(ls)).astype(BF16)
    rope_s = jnp.asarray(_rope_tables(ls))

    caches = [t.reshape(bs, DEPTH, lc, KV_W)
              for t in (cache_attn_c_k, cache_attn_c_v, cache_attn_d_k, cache_attn_d_v)]

    fold = max(1, min(bp, 1024 // lp))
    y_p, y_s = x_prompt, x_sample
    new_kv = [[], [], [], []]
    for layer in range(DEPTH):
        w_in_ctx = _permute_in_weight(w_in[layer], 512)
        w_in_lat = _permute_in_weight(w_in[layer], 1408)
        w_out_l = w_out[layer].astype(BF16).reshape(4, BRANCH_W, D_MODEL)
        qn2 = jnp.tile(q_norm[layer], LANES // HEAD_DIM).reshape(1, LANES)
        kn2 = jnp.tile(k_norm[layer], LANES // HEAD_DIM).reshape(1, LANES)
        sink = sink_logit[layer]

        xf = y_p.reshape(bp // fold, fold * lp, D_MODEL)
        h = _in_projection(xf, mod[layer], ctx_row, w_in_ctx).reshape(bp, lp, IN_WIDTH)
        ya = _fourier_branch(h, w_cs[layer], pos_p)
        yb = _hyena_branch(h, conv_w[layer], conv_b[layer], filt_p, layer, hyena_skip[layer], fwd_p, inv_p)
        oc, od, ck, cv, dk, dv = _attention_ctx(h, sink, qn2, kn2)
        for lst, t in zip(new_kv, (ck, cv, dk, dv)):
            lst.append(t.reshape(bp, lp, N_KV_HEADS, HEAD_DIM))
        parts = [t.reshape(bp // fold, fold * lp, BRANCH_W) for t in (ya, yb, oc, od)]
        y_p = _out_projection(parts, xf, mod[layer], ctx_row, w_out_l, ln_g[layer], ln_b[layer])
        y_p = y_p.reshape(bp, lp, D_MODEL)

        h = _in_projection(y_s, mod[layer], lat_row, w_in_lat)
        ya = _fourier_branch(h, w_cs[layer], pos_s)
        yb = _hyena_branch(h, conv_w[layer], conv_b[layer], filt_s, layer, hyena_skip[layer], fwd_s, inv_s)
        oc = _attention_window(h, caches[0], caches[1], layer, sink, rope_s)
        od = _attention_full(h, caches[2], caches[3], layer, rope_s, qn2, kn2)
        y_s = _out_projection([ya, yb, oc, od], y_s, mod[layer], lat_row, w_out_l, ln_g[layer], ln_b[layer])

    return (y_p, y_s) + tuple(jnp.stack(lst, axis=1) for lst in new_kv)
```

```python
import functools

import numpy as np
import jax
import jax.numpy as jnp
from jax import lax
from jax.experimental import pallas as pl
from jax.experimental.pallas import tpu as pltpu

D_MODEL = 2048
DEPTH = 2
GRID_W = 64
HEAD_DIM = 64
BRANCH_W = 512
FOURIER_GROUPS = 8
FOURIER_GROUP_W = BRANCH_W // FOURIER_GROUPS
FILTER_FREQS = 16
FILTER_FEATS = 1 + 2 * FILTER_FREQS
FILTER_HIDDEN = 64
DECAY_SHIFT = 0.05
N_HEADS = 8
N_KV_HEADS = 2
GROUP = N_HEADS // N_KV_HEADS
KV_W = N_KV_HEADS * HEAD_DIM
WINDOW = 128
ROPE_THETA = 10000.0
ALPHA = (2 * DEPTH) ** 0.25
LN_EPS = 1e-5
RMS_EPS = 1e-6
NEG_INF = -1e30
IN_WIDTH = 5632
SCALE = HEAD_DIM ** -0.5
LOG2E = 1.4426950408889634
QK_SCALE = SCALE * LOG2E
KEY_TILE = 256

F32 = jnp.float32
BF16 = jnp.bfloat16

VMEM_LIMIT_BYTES = 56 * 1024 * 1024
LANES = 128

COL_A_IN, COL_A_G, COL_B_V, COL_B_X1, COL_B_X2, COL_B_G = 0, 1, 2, 3, 4, 5
COL_CQ, COL_CG, COL_DQ, COL_DG, COL_KV = 6, 7, 8, 9, 10
MOD_ROWS = 16


def _cparams(*sem):
    return pltpu.CompilerParams(dimension_semantics=sem, vmem_limit_bytes=VMEM_LIMIT_BYTES)


def _silu(x):
    return x * jax.nn.sigmoid(x)


def _dot(a, b):
    return jnp.dot(a, b, preferred_element_type=F32)


def _dot_exact(a, b):
    return jnp.dot(a, b, preferred_element_type=F32, precision=lax.Precision.HIGHEST)


@functools.lru_cache(maxsize=None)
def _fourier_pos_table(l):
    k = np.arange(l, dtype=np.int64)
    ang = 2.0 * np.pi * ((k[:, None] * k[None, :]) % l) / l
    return (np.concatenate([np.cos(ang), -np.sin(ang)], axis=1) / np.sqrt(l)).astype(np.float32)


@functools.lru_cache(maxsize=None)
def _fourier_group_tables():
    w = FOURIER_GROUP_W
    k = np.arange(w, dtype=np.int64)
    ang = 2.0 * np.pi * ((k[:, None] * k[None, :]) % w) / w
    eye = np.eye(FOURIER_GROUPS)
    c = np.kron(eye, np.cos(ang)) / np.sqrt(w)
    s = np.kron(eye, np.sin(ang)) / np.sqrt(w)
    return np.stack([c, s]).astype(np.float32)


@functools.lru_cache(maxsize=None)
def _conv_dft_tables(l):
    n = 2 * l
    k = np.arange(l, dtype=np.int64)
    s = np.arange(l, dtype=np.int64)
    ang = 2.0 * np.pi * ((k[:, None] * s[None, :]) % n) / n
    fr = np.cos(ang)
    fi = -np.sin(ang)
    fi[0, :] = np.where(s % 2 == 0, 1.0, -1.0)
    fwd = np.concatenate([fr, fi], axis=0)
    t = np.arange(l, dtype=np.int64) + l // 2
    ang_i = 2.0 * np.pi * ((t[:, None] * k[None, :]) % n) / n
    wk = np.where(k == 0, 1.0, 2.0) / n
    gr = np.cos(ang_i) * wk[None, :]
    gi = -np.sin(ang_i) * wk[None, :]
    gi[:, 0] = np.where(t % 2 == 0, 1.0, -1.0) / n
    inv = np.concatenate([gr, gi], axis=1)
    return fwd.astype(np.float32), inv.astype(np.float32)


@functools.lru_cache(maxsize=None)
def _filter_feats(l):
    t = np.arange(l, dtype=np.float64) / l
    f = np.arange(1, FILTER_FREQS + 1, dtype=np.float64)
    ang = 2.0 * np.pi * t[:, None] * f[None, :]
    feats = np.concatenate([t[:, None], np.cos(ang), np.sin(ang)], axis=1)
    out = np.zeros((l, FILTER_HIDDEN), np.float64)
    out[:, :FILTER_FEATS] = feats
    return out.astype(np.float32)


@functools.lru_cache(maxsize=None)
def _rope_tables(l):
    t = np.arange(l)
    row = (t // GRID_W).astype(np.float64)
    col = (t % GRID_W).astype(np.float64)
    half = HEAD_DIM // 2
    inv = ROPE_THETA ** (-np.arange(0, half, 2, dtype=np.float64) / half)
    ar = row[:, None] * inv
    ac = col[:, None] * inv
    cos_h = np.concatenate([np.cos(ar), np.cos(ar), np.cos(ac), np.cos(ac)], axis=1)
    zero = np.zeros_like(ar)
    sin_up = np.concatenate([-np.sin(ar), zero, -np.sin(ac), zero], axis=1)
    sin_dn = np.concatenate([zero, np.sin(ar), zero, np.sin(ac)], axis=1)
    tabs = [np.tile(x, (1, LANES // HEAD_DIM)) for x in (cos_h, sin_up, sin_dn)]
    return np.stack(tabs).astype(np.float32)


def _mod_kernel(cond_ref, w_ref, b_ref, o_ref):
    s = _silu(cond_ref[...])
    o_ref[0] = _dot(s.astype(BF16), w_ref[0].astype(BF16)) + b_ref[0]


def _modulation(cond, w_ada, b_ada):
    tn = 512
    n = 3 * D_MODEL
    out = pl.pallas_call(
        _mod_kernel,
        grid=(DEPTH, n // tn),
        in_specs=[
            pl.BlockSpec((MOD_ROWS, D_MODEL), lambda l, j: (0, 0)),
            pl.BlockSpec((1, D_MODEL, tn), lambda l, j: (l, 0, j)),
            pl.BlockSpec((1, 1, tn), lambda l, j: (l, 0, j)),
        ],
        out_specs=pl.BlockSpec((1, MOD_ROWS, tn), lambda l, j: (l, 0, j)),
        out_shape=jax.ShapeDtypeStruct((DEPTH, MOD_ROWS, n), F32),
        compiler_params=_cparams("arbitrary", "arbitrary"),
        name="modulation",
    )(cond, w_ada, b_ada.reshape(DEPTH, 1, n))
    return out.reshape(DEPTH, MOD_ROWS, 3, D_MODEL)


def _layer_norm_rows(x):
    mu = jnp.mean(x, axis=-1, keepdims=True)
    xc = x - mu
    var = jnp.mean(xc * xc, axis=-1, keepdims=True)
    return xc * lax.rsqrt(var + LN_EPS)


def _inproj_kernel(x_ref, mod_ref, w_ref, o_ref, hm_ref):
    @pl.when(pl.program_id(2) == 0)
    def _():
        y = _layer_norm_rows(x_ref[0])
        hm = y * (1.0 + mod_ref[0, 1:2, :]) + mod_ref[0, 0:1, :]
        hm_ref[...] = hm.astype(BF16)

    o_ref[0] = _dot(hm_ref[...], w_ref[...])


def _in_projection(x, mod, mod_row, w_bf16, tl, tn):
    b, l, _ = x.shape
    return pl.pallas_call(
        _inproj_kernel,
        grid=(b, l // tl, IN_WIDTH // tn),
        in_specs=[
            pl.BlockSpec((1, tl, D_MODEL), lambda bi, i, j: (bi, i, 0)),
            pl.BlockSpec((1, 3, D_MODEL), lambda bi, i, j: (mod_row(bi), 0, 0)),
            pl.BlockSpec((D_MODEL, tn), lambda bi, i, j: (0, j)),
        ],
        out_specs=pl.BlockSpec((1, tl, tn), lambda bi, i, j: (bi, i, j)),
        out_shape=jax.ShapeDtypeStruct((b, l, IN_WIDTH), F32),
        scratch_shapes=[pltpu.VMEM((tl, D_MODEL), BF16)],
        compiler_params=_cparams("arbitrary", "arbitrary", "arbitrary"),
        name="in_projection",
    )(x, mod, w_bf16)


def _outproj_kernel(ya_ref, yb_ref, oc_ref, od_ref, x_ref, mod_ref, w_ref, g_ref, b_ref, o_ref):
    y = _dot(ya_ref[0], w_ref[0])
    y += _dot(yb_ref[0], w_ref[1])
    y += _dot(oc_ref[0], w_ref[2])
    y += _dot(od_ref[0], w_ref[3])
    z = ALPHA * x_ref[0] + mod_ref[0, 2:3, :] * y
    o_ref[0] = _layer_norm_rows(z) * g_ref[...] + b_ref[...]


def _out_projection(parts, x, mod, mod_row, w_out_bf16, ln_g, ln_b):
    b, l, _ = x.shape
    tl = min(l, 512)
    part_spec = pl.BlockSpec((1, tl, BRANCH_W), lambda bi, i: (bi, i, 0))
    row_spec = pl.BlockSpec((1, tl, D_MODEL), lambda bi, i: (bi, i, 0))
    vec_spec = pl.BlockSpec((1, D_MODEL), lambda bi, i: (0, 0))
    return pl.pallas_call(
        _outproj_kernel,
        grid=(b, l // tl),
        in_specs=[
            part_spec, part_spec, part_spec, part_spec, row_spec,
            pl.BlockSpec((1, 3, D_MODEL), lambda bi, i: (mod_row(bi), 0, 0)),
            pl.BlockSpec((4, BRANCH_W, D_MODEL), lambda bi, i: (0, 0, 0)),
            vec_spec, vec_spec,
        ],
        out_specs=row_spec,
        out_shape=jax.ShapeDtypeStruct((b, l, D_MODEL), F32),
        compiler_params=_cparams("arbitrary", "arbitrary"),
        name="out_projection",
    )(*parts, x, mod, w_out_bf16, ln_g.reshape(1, D_MODEL), ln_b.reshape(1, D_MODEL))


def _fourier_weight_kernel(t_ref, w_ref, o_ref):
    w = w_ref[0]
    o_ref[0, :, :BRANCH_W] = _dot_exact(t_ref[0], w).astype(BF16)
    o_ref[0, :, BRANCH_W:] = _dot_exact(t_ref[1], w).astype(BF16)


def _fourier_weights(w_fourier):
    return pl.pallas_call(
        _fourier_weight_kernel,
        grid=(DEPTH,),
        in_specs=[
            pl.BlockSpec((2, BRANCH_W, BRANCH_W), lambda l: (0, 0, 0)),
            pl.BlockSpec((1, BRANCH_W, BRANCH_W), lambda l: (l, 0, 0)),
        ],
        out_specs=pl.BlockSpec((1, BRANCH_W, 2 * BRANCH_W), lambda l: (l, 0, 0)),
        out_shape=jax.ShapeDtypeStruct((DEPTH, BRANCH_W, 2 * BRANCH_W), BF16),
        compiler_params=_cparams("arbitrary"),
        name="fourier_weights",
    )(jnp.asarray(_fourier_group_tables()), w_fourier)


def _fourier_kernel(a_ref, g_ref, w_ref, t_ref, o_ref):
    pq = _dot(a_ref[0].astype(BF16), w_ref[...])
    pq = jnp.concatenate([pq[:, :BRANCH_W], pq[:, BRANCH_W:]], axis=0).astype(BF16)
    y = _dot(t_ref[...], pq)
    o_ref[0] = (y * _silu(g_ref[0])).astype(BF16)


def _fourier_branch(h, w_cs, pos_table_bf16):
    b, l, _ = h.shape
    return pl.pallas_call(
        _fourier_kernel,
        grid=(b,),
        in_specs=[
            pl.BlockSpec((1, l, BRANCH_W), lambda bi: (bi, 0, COL_A_IN)),
            pl.BlockSpec((1, l, BRANCH_W), lambda bi: (bi, 0, COL_A_G)),
            pl.BlockSpec((BRANCH_W, 2 * BRANCH_W), lambda bi: (0, 0)),
            pl.BlockSpec((l, 2 * l), lambda bi: (0, 0)),
        ],
        out_specs=pl.BlockSpec((1, l, BRANCH_W), lambda bi: (bi, 0, 0)),
        out_shape=jax.ShapeDtypeStruct((b, l, BRANCH_W), BF16),
        compiler_params=_cparams("arbitrary"),
        name="fourier_branch",
    )(h, h, w_cs, pos_table_bf16)


def _hyena_filter_kernel(feat_ref, w1_ref, b1_ref, w2_ref, b2_ref, w3_ref, dec_ref, fhi_ref, flo_ref, o_ref,
                         hid_ref):
    l = feat_ref.shape[0]

    @pl.when(pl.program_id(1) == 0)
    def _():
        h1 = jnp.sin(_dot_exact(feat_ref[...], w1_ref[0]) + b1_ref[0])
        hid_ref[...] = jnp.sin(_dot_exact(h1, w2_ref[0]) + b2_ref[0])

    h = _dot_exact(hid_ref[...], w3_ref[0])
    t = lax.broadcasted_iota(jnp.int32, h.shape, 0)
    dist = jnp.abs(t - l // 2).astype(F32) * (1.0 / (l / 2))
    window = jnp.exp(-dist * jnp.exp(dec_ref[0])) + DECAY_SHIFT
    h = h * window
    h = h * lax.rsqrt(jnp.sum(h * h, axis=0, keepdims=True) + RMS_EPS)
    h_hi = h.astype(BF16)
    h_lo = (h - h_hi.astype(F32)).astype(BF16)
    spec = _dot(fhi_ref[...], h_hi) + (_dot(fhi_ref[...], h_lo) + _dot(flo_ref[...], h_hi))
    hr = spec[:l]
    hi = spec[l:]
    first = t == 0
    o_ref[0, 0] = hr
    o_ref[0, 1] = jnp.where(first, 0.0, hi)
    o_ref[0, 2] = jnp.where(first, hi, hr)


def _hyena_filters(l, fw1, fb1, fw2, fb2, fw3, log_decay, fwd_hi, fwd_lo):
    c = 2 * BRANCH_W
    tc = 256
    w1 = jnp.pad(fw1, ((0, 0), (0, FILTER_HIDDEN - FILTER_FEATS), (0, 0)))
    vec = lambda n: pl.BlockSpec((1, 1, n), lambda d, j: (d, 0, 0))
    mat = lambda m, n: pl.BlockSpec((1, m, n), lambda d, j: (d, 0, 0))
    return pl.pallas_call(
        _hyena_filter_kernel,
        grid=(DEPTH, c // tc),
        in_specs=[
            pl.BlockSpec((l, FILTER_HIDDEN), lambda d, j: (0, 0)),
            mat(FILTER_HIDDEN, FILTER_HIDDEN), vec(FILTER_HIDDEN),
            mat(FILTER_HIDDEN, FILTER_HIDDEN), vec(FILTER_HIDDEN),
            pl.BlockSpec((1, FILTER_HIDDEN, tc), lambda d, j: (d, 0, j)),
            pl.BlockSpec((1, 1, tc), lambda d, j: (d, 0, j)),
            pl.BlockSpec((2 * l, l), lambda d, j: (0, 0)),
            pl.BlockSpec((2 * l, l), lambda d, j: (0, 0)),
        ],
        out_specs=pl.BlockSpec((1, 3, l, tc), lambda d, j: (d, 0, 0, j)),
        out_shape=jax.ShapeDtypeStruct((DEPTH, 3, l, c), F32),
        scratch_shapes=[pltpu.VMEM((l, FILTER_HIDDEN), F32)],
        compiler_params=_cparams("arbitrary", "arbitrary"),
        name="hyena_filters",
    )(jnp.asarray(_filter_feats(l)), w1, fb1.reshape(DEPTH, 1, -1), fw2, fb2.reshape(DEPTH, 1, -1),
      fw3, log_decay.reshape(DEPTH, 1, c), fwd_hi, fwd_lo)


def _short_conv(u, w, b):
    l = u.shape[0]
    t = lax.broadcasted_iota(jnp.int32, u.shape, 0)
    prev = jnp.where(t == 0, 0.0, pltpu.roll(u, 1, axis=0))
    nxt = jnp.where(t == l - 1, 0.0, pltpu.roll(u, l - 1, axis=0))
    return prev * w[0:1] + u * w[1:2] + nxt * w[2:3] + b


def _long_conv(u, filt_ref, cols, skip, fwd_ref, inv_ref):
    l = u.shape[0]
    spec = _dot(fwd_ref[...], u.astype(BF16))
    ur, ui = spec[:l], spec[l:]
    hr, hi, hn = filt_ref[0, 0, :, cols], filt_ref[0, 1, :, cols], filt_ref[0, 2, :, cols]
    zr = ur * hr - ui * hi
    zi = ur * hi + ui * hn
    z = jnp.concatenate([zr, zi], axis=0).astype(BF16)
    return _dot(inv_ref[...], z) + u * skip


HYENA_CHAIN_W = 512


def _hyena_kernel(v_ref, x1_ref, x2_ref, g_ref, cwv_ref, cwx1_ref, cwx2_ref, cbv_ref, cbx1_ref,
                  cbx2_ref, f1_ref, f2_ref, s1_ref, s2_ref, fwd_ref, inv_ref, o_ref):
    for k in range(o_ref.shape[2] // HYENA_CHAIN_W):
        cs = slice(k * HYENA_CHAIN_W, (k + 1) * HYENA_CHAIN_W)
        v = _short_conv(v_ref[0, :, cs], cwv_ref[:, cs], cbv_ref[:, cs])
        x1 = _short_conv(x1_ref[0, :, cs], cwx1_ref[:, cs], cbx1_ref[:, cs])
        z = x1 * _long_conv(v, f1_ref, cs, s1_ref[:, cs], fwd_ref, inv_ref)
        x2 = _short_conv(x2_ref[0, :, cs], cwx2_ref[:, cs], cbx2_ref[:, cs])
        y = x2 * _long_conv(z, f2_ref, cs, s2_ref[:, cs], fwd_ref, inv_ref)
        o_ref[0, :, cs] = (y * _silu(g_ref[0, :, cs])).astype(BF16)


def _hyena_branch(h, conv_w, conv_b, filt, layer, skip, fwd_bf16, inv_bf16):
    b, l, _ = h.shape
    w = BRANCH_W
    once = pl.Buffered(1)
    col = lambda blk: pl.BlockSpec((1, l, w), lambda bi: (bi, 0, blk))
    cw = lambda part: pl.BlockSpec((3, w), lambda bi: (0, part), pipeline_mode=once)
    cb = lambda part: pl.BlockSpec((1, w), lambda bi: (0, part), pipeline_mode=once)
    flt = lambda which: pl.BlockSpec((1, 3, l, w), lambda bi: (layer, 0, 0, which), pipeline_mode=once)
    skp = lambda which: pl.BlockSpec((1, w), lambda bi: (0, which), pipeline_mode=once)
    conv_b2 = conv_b.reshape(1, -1)
    skip2 = skip.reshape(1, -1)
    return pl.pallas_call(
        _hyena_kernel,
        grid=(b,),
        in_specs=[
            col(COL_B_V), col(COL_B_X1), col(COL_B_X2), col(COL_B_G),
            cw(0), cw(1), cw(2), cb(0), cb(1), cb(2),
            flt(0), flt(1), skp(0), skp(1),
            pl.BlockSpec((2 * l, l), lambda bi: (0, 0), pipeline_mode=once),
            pl.BlockSpec((l, 2 * l), lambda bi: (0, 0), pipeline_mode=once),
        ],
        out_specs=pl.BlockSpec((1, l, w), lambda bi: (bi, 0, 0)),
        out_shape=jax.ShapeDtypeStruct((b, l, w), BF16),
        compiler_params=_cparams("arbitrary"),
        name="hyena_branch",
    )(h, h, h, h, conv_w, conv_w, conv_w, conv_b2, conv_b2, conv_b2,
      filt, filt, skip2, skip2, fwd_bf16, inv_bf16)


def _lane_lo(shape):
    return lax.broadcasted_iota(jnp.int32, shape, len(shape) - 1) < HEAD_DIM


def _rms_heads(x, g):
    lo = _lane_lo(x.shape)
    sq = x * x
    s_lo = jnp.sum(jnp.where(lo, sq, 0.0), axis=-1, keepdims=True)
    s_hi = jnp.sum(jnp.where(lo, 0.0, sq), axis=-1, keepdims=True)
    ms = jnp.where(lo, s_lo, s_hi) * (1.0 / HEAD_DIM)
    return x * lax.rsqrt(ms + RMS_EPS) * g


def _rope(x, tab_ref, rows=slice(None)):
    up = pltpu.roll(x, LANES - HEAD_DIM // 4, axis=1)
    dn = pltpu.roll(x, HEAD_DIM // 4, axis=1)
    return x * tab_ref[0, rows] + up * tab_ref[1, rows] + dn * tab_ref[2, rows]


def _dup_heads(x):
    lo = _lane_lo(x.shape)
    sw = pltpu.roll(x, HEAD_DIM, axis=1)
    return jnp.where(lo, x, sw).astype(BF16), jnp.where(lo, sw, x).astype(BF16)


def _stack_group(chunks):
    parts = []
    for x in chunks:
        lo = _lane_lo(x.shape)
        parts.append(jnp.where(lo, x, 0.0).astype(BF16))
        parts.append(jnp.where(lo, 0.0, x).astype(BF16))
    return jnp.concatenate(parts, axis=0)


def _nt_dot(q, k):
    return lax.dot_general(q, k, (((1,), (1,)), ((), ())), preferred_element_type=F32)


def _row_max(scores):
    by_width = {}
    for s in scores:
        by_width.setdefault(s.shape[1], []).append(s)
    ms = [jnp.max(functools.reduce(jnp.maximum, group), axis=-1, keepdims=True) for group in by_width.values()]
    return functools.reduce(jnp.maximum, ms)


def _group_attention(q4, kv_tiles, sinks, tq):
    scores = [_nt_dot(q4, k) for k, _, _ in kv_tiles]
    o_r = []
    for r in range(GROUP):
        rows = slice(r * tq, (r + 1) * tq)
        s_r = [s[rows] if bias is None else s[rows] + bias for s, (_, _, bias) in zip(scores, kv_tiles)]
        m = _row_max(s_r)
        if sinks is not None:
            m = jnp.maximum(m, sinks[r])
        p_r = [jnp.exp2(s - m) for s in s_r]
        den = functools.reduce(lambda a, c: a + c, [jnp.sum(p, axis=-1, keepdims=True) for p in p_r])
        if sinks is not None:
            den = den + jnp.exp2(sinks[r] - m)
        acc = None
        for p, (_, v, _) in zip(p_r, kv_tiles):
            d = _dot(p.astype(BF16), v)
            acc = d if acc is None else acc + d
        o_r.append(acc * (1.0 / den))
    lo = _lane_lo(o_r[0].shape)
    return [jnp.where(lo, o_r[2 * c], o_r[2 * c + 1]) for c in range(2)]


def _attend_groups(prep_q, g_ref, o_ref, kv_tiles_of_group, sinks_of_group, tq, rows=slice(None)):
    for j in range(N_KV_HEADS):
        q4 = _stack_group([prep_q(2 * j + c) for c in range(2)])
        outs = _group_attention(q4, kv_tiles_of_group(j), sinks_of_group(j), tq)
        for c in range(2):
            cols = slice((2 * j + c) * LANES, (2 * j + c + 1) * LANES)
            o_ref[0, rows, cols] = (outs[c] * _silu(g_ref[0, rows, cols])).astype(BF16)


def _sinks(sink_ref, j):
    return [sink_ref[GROUP * j + r] * LOG2E for r in range(GROUP)]


def _key_tiles(n):
    return [(t, min(KEY_TILE, n - t)) for t in range(0, n, KEY_TILE)]


def _attn_ctx_kernel(sink_ref, qc_ref, gc_ref, qd_ref, gd_ref, kv_ref, qn_ref, kn_ref,
                     oc_ref, od_ref, ck_ref, cv_ref, dk_ref, dv_ref):
    tq = qc_ref.shape[1]
    kv = kv_ref[0]
    ck, cv = kv[:, 0:KV_W], kv[:, KV_W:2 * KV_W]
    dk = _rms_heads(kv[:, 2 * KV_W:3 * KV_W], kn_ref[...])
    dv = kv[:, 3 * KV_W:]
    ck_ref[0], cv_ref[0], dk_ref[0], dv_ref[0] = ck, cv, dk, dv

    ckd, cvd = _dup_heads(ck), _dup_heads(cv)
    _attend_groups(lambda c: qc_ref[0, :, c * LANES:(c + 1) * LANES] * QK_SCALE, gc_ref, oc_ref,
                   lambda j: [(ckd[j], cvd[j], None)], lambda j: _sinks(sink_ref, j), tq)
    dkd, dvd = _dup_heads(dk), _dup_heads(dv)
    _attend_groups(lambda c: _rms_heads(qd_ref[0, :, c * LANES:(c + 1) * LANES], qn_ref[...]) * QK_SCALE,
                   gd_ref, od_ref, lambda j: [(dkd[j], dvd[j], None)], lambda j: None, tq)


def _attention_ctx(h, sink, qn2, kn2):
    b, l, _ = h.shape
    col = lambda blk: pl.BlockSpec((1, l, BRANCH_W), lambda bi: (bi, 0, blk))
    vec = pl.BlockSpec((1, LANES), lambda bi: (0, 0))
    o_spec = pl.BlockSpec((1, l, BRANCH_W), lambda bi: (bi, 0, 0))
    kv_spec = pl.BlockSpec((1, l, KV_W), lambda bi: (bi, 0, 0))
    kv_shape = jax.ShapeDtypeStruct((b, l, KV_W), F32)
    o_shape = jax.ShapeDtypeStruct((b, l, BRANCH_W), BF16)
    return pl.pallas_call(
        _attn_ctx_kernel,
        grid=(b,),
        in_specs=[
            pl.BlockSpec(memory_space=pltpu.SMEM),
            col(COL_CQ), col(COL_CG), col(COL_DQ), col(COL_DG), col(COL_KV), vec, vec,
        ],
        out_specs=[o_spec, o_spec, kv_spec, kv_spec, kv_spec, kv_spec],
        out_shape=[o_shape, o_shape, kv_shape, kv_shape, kv_shape, kv_shape],
        compiler_params=_cparams("arbitrary"),
        name="attention_context",
    )(sink, h, h, h, h, h, qn2, kn2)


def _attn_win_kernel(sink_ref, q_ref, g_ref, kv_ref, ck_ref, cv_ref, rope_ref,
                     o_ref, k_scr, v_scr, kc_scr, vc_scr, *, l, tq):
    i = pl.program_id(1)
    nblk = q_ref.shape[1] // tq

    @pl.when(i == 0)
    def _():
        zeros = jnp.zeros((WINDOW, LANES), BF16)
        for scr, val in ((k_scr, _dup_heads(_rope(kv_ref[0, :, 0:KV_W], rope_ref))),
                         (v_scr, _dup_heads(kv_ref[0, :, KV_W:2 * KV_W]))):
            for j in range(N_KV_HEADS):
                scr[j, 0:WINDOW, :] = zeros
                scr[j, WINDOW:WINDOW + l, :] = val[j]
                scr[j, WINDOW + l:, :] = zeros
        for scr, val in ((kc_scr, _dup_heads(ck_ref[0, 0])), (vc_scr, _dup_heads(cv_ref[0, 0]))):
            for j in range(N_KV_HEADS):
                scr[j] = val[j]

    span = tq + 2 * WINDOW
    r = lax.broadcasted_iota(jnp.int32, (tq, span), 0)
    c = lax.broadcasted_iota(jnp.int32, (tq, span), 1)
    band = (c >= r) & (c <= r + 2 * WINDOW)
    for blk in range(nblk):
        q0 = (i * nblk + blk) * tq
        base = pl.multiple_of(q0, tq)
        rows = slice(blk * tq, (blk + 1) * tq)
        trows = pl.ds(base, tq)
        kpos = c + (q0 - WINDOW)
        bias = jnp.where(band & (kpos >= 0) & (kpos < l), 0.0, NEG_INF)

        def kv_tiles(j, base=base, bias=bias):
            tiles = [(k_scr[j, pl.ds(base + t, n), :], v_scr[j, pl.ds(base + t, n), :], bias[:, t:t + n])
                     for t, n in _key_tiles(span)]
            return tiles + [(kc_scr[j, t:t + n, :], vc_scr[j, t:t + n, :], None)
                            for t, n in _key_tiles(kc_scr.shape[1])]

        def prep_q(ch, rows=rows, trows=trows):
            return _rope(q_ref[0, rows, ch * LANES:(ch + 1) * LANES], rope_ref, trows) * QK_SCALE

        _attend_groups(prep_q, g_ref, o_ref, kv_tiles, lambda j: _sinks(sink_ref, j), tq, rows)


def _attention_window(h, ctx_k, ctx_v, layer, sink, rope_tab):
    b, l, _ = h.shape
    tq = WINDOW
    tb = 2 * tq
    lc = ctx_k.shape[2]
    ctx_spec = pl.BlockSpec((1, 1, lc, KV_W), lambda bi, i: (bi, layer, 0, 0))
    return pl.pallas_call(
        functools.partial(_attn_win_kernel, l=l, tq=tq),
        grid=(b, l // tb),
        in_specs=[
            pl.BlockSpec(memory_space=pltpu.SMEM),
            pl.BlockSpec((1, tb, BRANCH_W), lambda bi, i: (bi, i, COL_CQ)),
            pl.BlockSpec((1, tb, BRANCH_W), lambda bi, i: (bi, i, COL_CG)),
            pl.BlockSpec((1, l, BRANCH_W), lambda bi, i: (bi, 0, COL_KV)),
            ctx_spec, ctx_spec,
            pl.BlockSpec((3, l, LANES), lambda bi, i: (0, 0, 0)),
        ],
        out_specs=pl.BlockSpec((1, tb, BRANCH_W), lambda bi, i: (bi, i, 0)),
        out_shape=jax.ShapeDtypeStruct((b, l, BRANCH_W), BF16),
        scratch_shapes=[
            pltpu.VMEM((N_KV_HEADS, l + 2 * WINDOW, LANES), BF16),
            pltpu.VMEM((N_KV_HEADS, l + 2 * WINDOW, LANES), BF16),
            pltpu.VMEM((N_KV_HEADS, lc, LANES), BF16),
            pltpu.VMEM((N_KV_HEADS, lc, LANES), BF16),
        ],
        compiler_params=_cparams("arbitrary", "arbitrary"),
        name="attention_window",
    )(sink, h, h, h, ctx_k, ctx_v, rope_tab)


def _attn_full_kernel(q_ref, g_ref, kv_ref, ck_ref, cv_ref, rope_ref, qn_ref, kn_ref,
                      o_ref, k_scr, v_scr, *, l, tq):
    i = pl.program_id(1)
    nblk = q_ref.shape[1] // tq

    @pl.when(i == 0)
    def _():
        k = _rope(_rms_heads(kv_ref[0, :, 2 * KV_W:3 * KV_W], kn_ref[...]), rope_ref)
        for scr, loc, ctx in ((k_scr, _dup_heads(k), _dup_heads(ck_ref[0, 0])),
                              (v_scr, _dup_heads(kv_ref[0, :, 3 * KV_W:]), _dup_heads(cv_ref[0, 0]))):
            for j in range(N_KV_HEADS):
                scr[j, 0:l, :] = loc[j]
                scr[j, l:, :] = ctx[j]

    def kv_tiles(j):
        return [(k_scr[j, t:t + n, :], v_scr[j, t:t + n, :], None) for t, n in _key_tiles(k_scr.shape[1])]

    for blk in range(nblk):
        rows = slice(blk * tq, (blk + 1) * tq)
        trows = pl.ds(pl.multiple_of((i * nblk + blk) * tq, tq), tq)

        def prep_q(ch, rows=rows, trows=trows):
            qq = _rms_heads(q_ref[0, rows, ch * LANES:(ch + 1) * LANES], qn_ref[...])
            return _rope(qq, rope_ref, trows) * QK_SCALE

        _attend_groups(prep_q, g_ref, o_ref, kv_tiles, lambda j: None, tq, rows)


def _attention_full(h, ctx_k, ctx_v, layer, rope_tab, qn2, kn2):
    b, l, _ = h.shape
    tq = 256
    tb = 2 * tq
    lc = ctx_k.shape[2]
    ctx_spec = pl.BlockSpec((1, 1, lc, KV_W), lambda bi, i: (bi, layer, 0, 0))
    vec = pl.BlockSpec((1, LANES), lambda bi, i: (0, 0))
    return pl.pallas_call(
        functools.partial(_attn_full_kernel, l=l, tq=tq),
        grid=(b, l // tb),
        in_specs=[
            pl.BlockSpec((1, tb, BRANCH_W), lambda bi, i: (bi, i, COL_DQ)),
            pl.BlockSpec((1, tb, BRANCH_W), lambda bi, i: (bi, i, COL_DG)),
            pl.BlockSpec((1, l, BRANCH_W), lambda bi, i: (bi, 0, COL_KV)),
            ctx_spec, ctx_spec,
            pl.BlockSpec((3, l, LANES), lambda bi, i: (0, 0, 0)),
            vec, vec,
        ],
        out_specs=pl.BlockSpec((1, tb, BRANCH_W), lambda bi, i: (bi, i, 0)),
        out_shape=jax.ShapeDtypeStruct((b, l, BRANCH_W), BF16),
        scratch_shapes=[
            pltpu.VMEM((N_KV_HEADS, l + lc, LANES), BF16),
            pltpu.VMEM((N_KV_HEADS, l + lc, LANES), BF16),
        ],
        compiler_params=_cparams("arbitrary", "arbitrary"),
        name="attention_full",
    )(h, h, h, ctx_k, ctx_v, rope_tab, qn2, kn2)


def _permute_in_weight(w):
    return jnp.concatenate(
        [w[:, :3584], w[:, 3840:4864], w[:, 5120:5632], w[:, 3584:3840], w[:, 4864:5120]],
        axis=1).astype(BF16)


def kernel(x_prompt, x_sample, cache_attn_c_k, cache_attn_c_v, cache_attn_d_k, cache_attn_d_v, c, c_ctx, w_ada, b_ada, w_in, w_fourier, conv_w, conv_b, filt_w1, filt_b1, filt_w2, filt_b2, filt_w3, filt_log_decay, hyena_skip, sink_logit, q_norm, k_norm, w_out, ln_g, ln_b):
    bp, lp, _ = x_prompt.shape
    bs, ls, _ = x_sample.shape
    lc = cache_attn_c_k.shape[2]
    assert lp == lc and bs + 1 <= MOD_ROWS

    cond = jnp.zeros((MOD_ROWS, D_MODEL), F32).at[0].set(c_ctx).at[1:1 + bs].set(c)
    mod = _modulation(cond, w_ada, b_ada)
    ctx_row = lambda bi: 0
    lat_row = lambda bi: bi + 1

    w_cs = _fourier_weights(w_fourier)
    def conv_tables(l):
        fwd, inv = (jnp.asarray(t) for t in _conv_dft_tables(l))
        fwd_hi = fwd.astype(BF16)
        fwd_lo = (fwd - fwd_hi.astype(F32)).astype(BF16)
        return fwd_hi, fwd_lo, inv.astype(BF16)

    fwd_p, fwd_lo_p, inv_p = conv_tables(lp)
    fwd_s, fwd_lo_s, inv_s = conv_tables(ls)
    filt_p = _hyena_filters(lp, filt_w1, filt_b1, filt_w2, filt_b2, filt_w3, filt_log_decay, fwd_p, fwd_lo_p)
    filt_s = _hyena_filters(ls, filt_w1, filt_b1, filt_w2, filt_b2, filt_w3, filt_log_decay, fwd_s, fwd_lo_s)
    pos_p = jnp.asarray(_fourier_pos_table(lp)).astype(BF16)
    pos_s = jnp.asarray(_fourier_pos_table(ls)).astype(BF16)
    rope_s = jnp.asarray(_rope_tables(ls))

    caches = [t.reshape(bs, DEPTH, lc, KV_W)
              for t in (cache_attn_c_k, cache_attn_c_v, cache_attn_d_k, cache_attn_d_v)]

    fold = max(1, min(bp, 1024 // lp))
    y_p, y_s = x_prompt, x_sample
    new_kv = [[], [], [], []]
    for layer in range(DEPTH):
        w_in_l = _permute_in_weight(w_in[layer])
        w_out_l = w_out[layer].astype(BF16).reshape(4, BRANCH_W, D_MODEL)
        qn2 = jnp.tile(q_norm[layer], LANES // HEAD_DIM).reshape(1, LANES)
        kn2 = jnp.tile(k_norm[layer], LANES // HEAD_DIM).reshape(1, LANES)
        sink = sink_logit[layer]

        xf = y_p.reshape(bp // fold, fold * lp, D_MODEL)
        h = _in_projection(xf, mod[layer], ctx_row, w_in_l, 512, 2816).reshape(bp, lp, IN_WIDTH)
        ya = _fourier_branch(h, w_cs[layer], pos_p)
        yb = _hyena_branch(h, conv_w[layer], conv_b[layer], filt_p, layer, hyena_skip[layer], fwd_p, inv_p)
        oc, od, ck, cv, dk, dv = _attention_ctx(h, sink, qn2, kn2)
        for lst, t in zip(new_kv, (ck, cv, dk, dv)):
            lst.append(t.reshape(bp, lp, N_KV_HEADS, HEAD_DIM))
        parts = [t.reshape(bp // fold, fold * lp, BRANCH_W) for t in (ya, yb, oc, od)]
        y_p = _out_projection(parts, xf, mod[layer], ctx_row, w_out_l, ln_g[layer], ln_b[layer])
        y_p = y_p.reshape(bp, lp, D_MODEL)

        h = _in_projection(y_s, mod[layer], lat_row, w_in_l, 1024, 1408)
        ya = _fourier_branch(h, w_cs[layer], pos_s)
        yb = _hyena_branch(h, conv_w[layer], conv_b[layer], filt_s, layer, hyena_skip[layer], fwd_s, inv_s)
        oc = _attention_window(h, caches[0], caches[1], layer, sink, rope_s)
        od = _attention_full(h, caches[2], caches[3], layer, rope_s, qn2, kn2)
        y_s = _out_projection([ya, yb, oc, od], y_s, mod[layer], lat_row, w_out_l, ln_g[layer], ln_b[layer])

    return (y_p, y_s) + tuple(jnp.stack(lst, axis=1) for lst in new_kv)
```
